```python
import jax, jax.numpy as jnp
from jax import lax
import numpy as np

D_MODEL = 1024
BATCH = 16
SEQ = 4096
DEPTH = 2

GRID_W = 64
CTX_LEN = 256
HEAD_DIM = 64
ATTN_WIDTH = D_MODEL // 2
N_Q_HEADS = ATTN_WIDTH // HEAD_DIM
N_KV_HEADS = N_Q_HEADS // 4
GQA_GROUP = N_Q_HEADS // N_KV_HEADS
KV_WIDTH = N_KV_HEADS * HEAD_DIM
POOL_WINDOWS = (2, 4, 8, 16)
N_POOL_GROUPS = len(POOL_WINDOWS)
POOL_WIDTH = D_MODEL // 4
POOL_GROUP = POOL_WIDTH // N_POOL_GROUPS
N_FOURIER_GROUPS = 4
FOURIER_WIDTH = D_MODEL // 4
FOURIER_GROUP = FOURIER_WIDTH // N_FOURIER_GROUPS
MIX_WIDTH = ATTN_WIDTH + POOL_WIDTH + FOURIER_WIDTH
Q_END = ATTN_WIDTH
K_END = Q_END + KV_WIDTH
KV_END = K_END + KV_WIDTH
POOL_END = KV_END + POOL_WIDTH
IN_WIDTH = POOL_END + FOURIER_WIDTH
D_FF = -(-8 * D_MODEL // (3 * 256)) * 256
Q_BLOCK = 128
ROPE_THETA = 10000.0
EPS = 1e-6

kernel_name = "hybrid_attn_pool_fourier_dit_block"


def rms_norm(x, g):
    xf = x.astype(jnp.float32)
    y = xf * lax.rsqrt(jnp.mean(xf * xf, axis=-1, keepdims=True) + EPS)
    return (y * g.astype(jnp.float32)).astype(x.dtype)


def axial_rope_tables(length):
    rows = length // GRID_W
    row_ids = jnp.repeat(jnp.arange(rows), GRID_W).astype(jnp.float32)
    col_ids = jnp.tile(jnp.arange(GRID_W), rows).astype(jnp.float32)
    n_freq = HEAD_DIM // 4
    freqs = ROPE_THETA ** (-jnp.arange(n_freq, dtype=jnp.float32) / n_freq)
    ang = jnp.stack([row_ids[:, None] * freqs, col_ids[:, None] * freqs], axis=1)
    return jnp.cos(ang), jnp.sin(ang)


def apply_rope(x, cos, sin):
    n_freq = HEAD_DIM // 4
    xf = x.astype(jnp.float32).reshape(*x.shape[:-1], 2, 2, n_freq)
    x1, x2 = xf[..., 0, :], xf[..., 1, :]
    cb, sb = cos[None, :, None], sin[None, :, None]
    out = jnp.stack([x1 * cb - x2 * sb, x2 * cb + x1 * sb], axis=-2)
    return out.reshape(x.shape).astype(x.dtype)


def gqa_softmax(qg, k, v):
    s = jnp.einsum('bqgrd,bkgd->bgrqk', qg, k, preferred_element_type=jnp.float32) * (HEAD_DIM ** -0.5)
    p = jax.nn.softmax(s, axis=-1).astype(v.dtype)
    return jnp.einsum('bgrqk,bkgd->bqgrd', p, v)


def latent_attention(q, k, v, k_ctx, v_ctx):
    b, s = q.shape[:2]
    n_blk = s // Q_BLOCK
    k_all = jnp.concatenate([k_ctx, k], axis=1)
    v_all = jnp.concatenate([v_ctx, v], axis=1)
    qb = q.reshape(b, n_blk, Q_BLOCK, N_KV_HEADS, GQA_GROUP, HEAD_DIM).transpose(1, 0, 2, 3, 4, 5)
    o = lax.map(lambda qblk: gqa_softmax(qblk, k_all, v_all), qb)
    return o.transpose(1, 0, 2, 3, 4, 5).reshape(b, s, ATTN_WIDTH)


def context_attention(q, k, v):
    b, cl = q.shape[:2]
    qg = q.reshape(b, cl, N_KV_HEADS, GQA_GROUP, HEAD_DIM)
    return gqa_softmax(qg, k, v).reshape(b, cl, ATTN_WIDTH)


def pool_mix(u, w_pool, pool_scale):
    b, length, _ = u.shape
    ug = u.astype(jnp.float32).reshape(b, length, N_POOL_GROUPS, POOL_GROUP)
    cs = jnp.concatenate([jnp.zeros((b, 1, N_POOL_GROUPS, POOL_GROUP), jnp.float32),
                          jnp.cumsum(ug, axis=1)], axis=1)
    t = jnp.arange(length)
    means = []
    for gi, w in enumerate(POOL_WINDOWS):
        lo = jnp.clip(t - w // 2, 0, length)
        hi = jnp.clip(t + w - w // 2, 0, length)
        cs_g = cs[:, :, gi]
        cnt = (hi - lo).astype(jnp.float32)
        means.append((cs_g[:, hi] - cs_g[:, lo]) / cnt[None, :, None])
    d = jnp.stack(means, axis=2) - ug
    y = jnp.einsum('blgc,gcd->blgd', d, w_pool.astype(jnp.float32)).reshape(b, length, POOL_WIDTH)
    return (y * pool_scale.astype(jnp.float32)).astype(u.dtype)


def fourier_mix(u, w_four):
    b, length, _ = u.shape
    ug = u.astype(jnp.float32).reshape(b, length, N_FOURIER_GROUPS, FOURIER_GROUP)
    f = jnp.fft.fft2(ug, axes=(1, 3), norm='ortho').real
    y = jnp.einsum('blgc,gcd->blgd', f, w_four.astype(jnp.float32))
    return y.reshape(b, length, FOURIER_WIDTH).astype(u.dtype)


def head_rms(x, g):
    return rms_norm(x, g)


def swiglu(h, w_gate_up, w_down):
    gate, up = jnp.split(h @ w_gate_up, 2, axis=-1)
    return (jax.nn.silu(gate) * up) @ w_down


def setup_inputs(seed: int = 0) -> dict:
    key = jax.random.key(seed)
    ks = jax.random.split(key, 18)
    f32 = jnp.float32

    def nrm(k, shape, s):
        return jax.random.normal(k, shape, f32) * s

    return {
        "x": nrm(ks[0], (BATCH, SEQ, D_MODEL), 1.0),
        "c": nrm(ks[1], (BATCH, D_MODEL), 1.0),
        "ctx": nrm(ks[2], (BATCH, CTX_LEN, D_MODEL), 1.0),
        "c_ctx": nrm(ks[3], (D_MODEL,), 1.0),
        "w_ada": nrm(ks[4], (DEPTH, D_MODEL, 6 * D_MODEL), D_MODEL ** -0.5),
        "b_ada": nrm(ks[5], (DEPTH, 6 * D_MODEL), 0.01),
        "g_mix": 1.0 + nrm(ks[6], (DEPTH, D_MODEL), 0.02),
        "g_ffn": 1.0 + nrm(ks[7], (DEPTH, D_MODEL), 0.02),
        "w_in": nrm(ks[8], (DEPTH, D_MODEL, IN_WIDTH), D_MODEL ** -0.5),
        "q_gain": 1.0 + nrm(ks[9], (DEPTH, HEAD_DIM), 0.02),
        "k_gain": 1.0 + nrm(ks[10], (DEPTH, HEAD_DIM), 0.02),
        "w_pool": nrm(ks[11], (DEPTH, N_POOL_GROUPS, POOL_GROUP, POOL_GROUP), POOL_GROUP ** -0.5),
        "pool_scale": 1.0 + nrm(ks[12], (DEPTH, POOL_WIDTH), 0.02),
        "w_four": nrm(ks[13], (DEPTH, N_FOURIER_GROUPS, FOURIER_GROUP, FOURIER_GROUP), FOURIER_GROUP ** -0.5),
        "w_out": nrm(ks[14], (DEPTH, MIX_WIDTH, D_MODEL), MIX_WIDTH ** -0.5),
        "w_gate_up": nrm(ks[15], (DEPTH, D_MODEL, 2 * D_FF), D_MODEL ** -0.5),
        "w_down": nrm(ks[16], (DEPTH, D_FF, D_MODEL), D_FF ** -0.5),
    }


def reference(x, c, ctx, c_ctx, w_ada, b_ada, g_mix, g_ffn, w_in, q_gain, k_gain,
              w_pool, pool_scale, w_four, w_out, w_gate_up, w_down):
    b, s, _ = x.shape
    cl = ctx.shape[1]
    cos, sin = axial_rope_tables(s)
    sc, scc = jax.nn.silu(c), jax.nn.silu(c_ctx)
    for i in range(DEPTH):
        update_ctx = i < DEPTH - 1
        mod_x = (sc @ w_ada[i] + b_ada[i])[:, None, :]
        mod_c = scc @ w_ada[i] + b_ada[i]
        sh1, sc1, ga1, sh2, sc2, ga2 = jnp.split(mod_x, 6, axis=-1)
        csh1, csc1, cga1, csh2, csc2, cga2 = jnp.split(mod_c, 6, axis=-1)

        hx = rms_norm(x, g_mix[i]) * (1.0 + sc1) + sh1
        hc = rms_norm(ctx, g_mix[i]) * (1.0 + csc1) + csh1
        zx = hx @ w_in[i]
        if update_ctx:
            zc = hc @ w_in[i]
            zc_kv = zc[..., Q_END:KV_END]
        else:
            zc_kv = hc @ w_in[i][:, Q_END:KV_END]
        kc = head_rms(zc_kv[..., :KV_WIDTH].reshape(b, cl, N_KV_HEADS, HEAD_DIM), k_gain[i])
        vc = zc_kv[..., KV_WIDTH:].reshape(b, cl, N_KV_HEADS, HEAD_DIM)

        q = apply_rope(head_rms(zx[..., :Q_END].reshape(b, s, N_Q_HEADS, HEAD_DIM), q_gain[i]), cos, sin)
        k = apply_rope(head_rms(zx[..., Q_END:K_END].reshape(b, s, N_KV_HEADS, HEAD_DIM), k_gain[i]), cos, sin)
        v = zx[..., K_END:KV_END].reshape(b, s, N_KV_HEADS, HEAD_DIM)
        mx = jnp.concatenate([
            latent_attention(q, k, v, kc, vc),
            pool_mix(zx[..., KV_END:POOL_END], w_pool[i], pool_scale[i]),
            fourier_mix(zx[..., POOL_END:], w_four[i]),
        ], axis=-1)
        x = x + ga1 * (mx @ w_out[i])

        if update_ctx:
            qc = head_rms(zc[..., :Q_END].reshape(b, cl, N_Q_HEADS, HEAD_DIM), q_gain[i])
            mc = jnp.concatenate([
                context_attention(qc, kc, vc),
                pool_mix(zc[..., KV_END:POOL_END], w_pool[i], pool_scale[i]),
                fourier_mix(zc[..., POOL_END:], w_four[i]),
            ], axis=-1)
            ctx = ctx + cga1 * (mc @ w_out[i])

        fx = rms_norm(x, g_ffn[i]) * (1.0 + sc2) + sh2
        x = x + ga2 * swiglu(fx, w_gate_up[i], w_down[i])
        if update_ctx:
            fc = rms_norm(ctx, g_ffn[i]) * (1.0 + csc2) + csh2
            ctx = ctx + cga2 * swiglu(fc, w_gate_up[i], w_down[i])
    return x
```

```python
import functools
import math

import jax
import jax.numpy as jnp
from jax import lax
from jax.experimental import pallas as pl
from jax.experimental.pallas import tpu as pltpu

F32 = jnp.float32
BF16 = jnp.bfloat16

GRID_W = 64
HEAD_DIM = 64
N_Q_HEADS = 8
N_KV_HEADS = 2
GQA_GROUP = N_Q_HEADS // N_KV_HEADS
Q_WIDTH = N_Q_HEADS * HEAD_DIM
KV_WIDTH = N_KV_HEADS * HEAD_DIM
POOL_WINDOWS = (2, 4, 8, 16)
POOL_WIDTH = 256
POOL_GROUP = 64
FOUR_WIDTH = 256
FOUR_GROUP = 64
N_FREQ = HEAD_DIM // 4
ROPE_THETA = 10000.0
EPS = 1e-6
POOL_HALO = max(POOL_WINDOWS) // 2

VMEM_LIMIT_BYTES = 56 * 1024 * 1024


def _params(*sem):
    return pltpu.CompilerParams(dimension_semantics=sem, vmem_limit_bytes=VMEM_LIMIT_BYTES)


def _resident(shape):
    zeros = (0,) * len(shape)
    return pl.BlockSpec(shape, lambda *_: zeros, pipeline_mode=pl.Buffered(1))


def _rms(x):
    return x * lax.rsqrt(jnp.mean(x * x, axis=-1, keepdims=True) + EPS)


def _ada_kernel(c_ref, w_ref, b_ref, o_ref):
    c = c_ref[...]
    sc = c / (1.0 + jnp.exp(-c))
    o_ref[0] = jnp.dot(sc, w_ref[0], preferred_element_type=F32,
                       precision=lax.Precision.HIGHEST) + b_ref[0]


def _ada(cvec, w_ada, b_ada):
    depth, d, n = w_ada.shape
    r = cvec.shape[0]
    tn = 1536
    return pl.pallas_call(
        _ada_kernel,
        out_shape=jax.ShapeDtypeStruct((depth, r, n), F32),
        grid=(depth, n // tn),
        in_specs=[pl.BlockSpec((r, d), lambda i, j: (0, 0)),
                  pl.BlockSpec((1, d, tn), lambda i, j: (i, 0, j)),
                  pl.BlockSpec((1, 1, tn), lambda i, j: (i, 0, j))],
        out_specs=pl.BlockSpec((1, r, tn), lambda i, j: (i, 0, j)),
        compiler_params=_params("arbitrary", "arbitrary"),
        name="ada_mod",
    )(cvec, w_ada, b_ada.reshape(depth, 1, n))


def _four_w_kernel(c_ref, s_ref, w_ref, gr_ref, gi_ref):
    for g in range(w_ref.shape[1]):
        w = w_ref[0, g]
        gr_ref[0, g] = jnp.dot(c_ref[...], w, preferred_element_type=F32, precision=lax.Precision.HIGHEST)
        gi_ref[0, g] = -jnp.dot(s_ref[...], w, preferred_element_type=F32, precision=lax.Precision.HIGHEST)


def _four_weights(w_four):
    depth, ng, gc, _ = w_four.shape
    c = jnp.arange(gc, dtype=jnp.int32)
    ang = ((c[:, None] * c[None, :]) % gc).astype(F32) * (2.0 * math.pi / gc)
    scale = gc ** -0.5
    spec = pl.BlockSpec((1, ng, gc, gc), lambda i: (i, 0, 0, 0))
    tab = pl.BlockSpec((gc, gc), lambda i: (0, 0))
    gr, gi = pl.pallas_call(
        _four_w_kernel,
        out_shape=[jax.ShapeDtypeStruct(w_four.shape, F32)] * 2,
        grid=(depth,),
        in_specs=[tab, tab, spec],
        out_specs=[spec, spec],
        compiler_params=_params("arbitrary"),
        name="four_weights",
    )(jnp.cos(ang) * scale, jnp.sin(ang) * scale, w_four)
    return gr, gi


def _block_diag(blocks):
    g, a, b = blocks.shape
    eye = jnp.eye(g, dtype=blocks.dtype)
    return (eye[:, None, :, None] * blocks[:, :, None, :]).reshape(g * a, g * b)


def _in_proj_kernel(x_ref, sh_ref, sc_ref, g_ref, wqv_ref, wr_ref, hm_ref, kg_ref, qtab_ref,
                    kcos_ref, ksin_ref, ab_ref, q_ref, k_ref, v_ref, up_ref, vf_ref, *, tq, tk):
    tm = x_ref.shape[1]
    h = _rms(x_ref[0]) * g_ref[...]
    h = h * (1.0 + sc_ref[0]) + sh_ref[0]
    hb = h.astype(BF16)
    zt = lax.dot_general(wqv_ref[...], hb, (((1,), (1,)), ((), ())), preferred_element_type=F32)
    zr = jnp.dot(hb, wr_ref[...], preferred_element_type=F32)

    tab = qtab_ref[...]
    for hd in range(N_Q_HEADS):
        qh = zt[hd * HEAD_DIM:(hd + 1) * HEAD_DIM, :]
        ss = jnp.sum(qh * qh, axis=0, keepdims=True) * (1.0 / HEAD_DIM)
        qn = qh * lax.rsqrt(ss + EPS)
        parts = []
        for seg in range(2):
            x1 = qn[seg * 32:seg * 32 + N_FREQ]
            x2 = qn[seg * 32 + N_FREQ:seg * 32 + 2 * N_FREQ]
            t0 = seg * 64
            g1c, g2s = tab[t0:t0 + 16], tab[t0 + 16:t0 + 32]
            g2c, g1s = tab[t0 + 32:t0 + 48], tab[t0 + 48:t0 + 64]
            parts += [x1 * g1c - x2 * g2s, x2 * g2c + x1 * g1s]
        qr = jnp.concatenate(parts, axis=0).astype(BF16)
        g, r = divmod(hd, GQA_GROUP)
        for j in range(tm // tq):
            q_ref[0, g, j, g * HEAD_DIM:(g + 1) * HEAD_DIM, r * tq:(r + 1) * tq] = qr[:, j * tq:(j + 1) * tq]
    for g in range(N_KV_HEADS):
        o = 1 - g
        for j in range(tm // tq):
            q_ref[0, g, j, o * HEAD_DIM:(o + 1) * HEAD_DIM, :] = jnp.zeros((HEAD_DIM, GQA_GROUP * tq), BF16)

    vt = zt[Q_WIDTH:Q_WIDTH + KV_WIDTH].astype(BF16)
    for g in range(N_KV_HEADS):
        for j in range(tm // tk):
            v_ref[0, g, j] = vt[g * HEAD_DIM:(g + 1) * HEAD_DIM, j * tk:(j + 1) * tk]

    kz = zr[:, :KV_WIDTH]
    ssk = jnp.dot((kz * kz).astype(BF16), hm_ref[...], preferred_element_type=F32)
    kn = kz * lax.rsqrt(ssk + EPS) * kg_ref[...]
    lane = lax.broadcasted_iota(jnp.int32, kn.shape, 1)
    rot = jnp.where((lane % 32) < N_FREQ,
                    pltpu.roll(kn, KV_WIDTH - N_FREQ, 1), pltpu.roll(kn, N_FREQ, 1))
    k_ref[0] = (kn * kcos_ref[...] + rot * ksin_ref[...]).astype(BF16)

    up_ref[0] = zr[:, KV_WIDTH:KV_WIDTH + POOL_WIDTH]
    uf = zr[:, KV_WIDTH + POOL_WIDTH:].astype(BF16)
    v2 = jnp.dot(uf, ab_ref[...], preferred_element_type=F32)
    vf_ref[0, 0] = v2[:, :FOUR_WIDTH].astype(BF16)
    vf_ref[0, 1] = v2[:, FOUR_WIDTH:].astype(BF16)


def _in_proj(x, sh, sc, g, wqv, wr, hm, kg, qtab, kcos, ksin, ab, *, tm, tq, tk):
    b, length, d = x.shape
    nt = length // tm
    bm = sh.shape[0]
    mod = pl.BlockSpec((1, 1, d), (lambda i, t: (i, 0, 0)) if bm > 1 else (lambda i, t: (0, 0, 0)))
    out_shape = [
        jax.ShapeDtypeStruct((b, N_KV_HEADS, length // tq, KV_WIDTH, GQA_GROUP * tq), BF16),
        jax.ShapeDtypeStruct((b, length, KV_WIDTH), BF16),
        jax.ShapeDtypeStruct((b, N_KV_HEADS, length // tk, HEAD_DIM, tk), BF16),
        jax.ShapeDtypeStruct((b, length, POOL_WIDTH), F32),
        jax.ShapeDtypeStruct((b, 2, length, FOUR_WIDTH), BF16),
    ]
    out_specs = [
        pl.BlockSpec((1, N_KV_HEADS, tm // tq, KV_WIDTH, GQA_GROUP * tq), lambda i, t: (i, 0, t, 0, 0)),
        pl.BlockSpec((1, tm, KV_WIDTH), lambda i, t: (i, t, 0)),
        pl.BlockSpec((1, N_KV_HEADS, tm // tk, HEAD_DIM, tk), lambda i, t: (i, 0, t, 0, 0)),
        pl.BlockSpec((1, tm, POOL_WIDTH), lambda i, t: (i, t, 0)),
        pl.BlockSpec((1, 2, tm, FOUR_WIDTH), lambda i, t: (i, 0, t, 0)),
    ]
    in_specs = [
        pl.BlockSpec((1, tm, d), lambda i, t: (i, t, 0)),
        mod, mod,
        _resident((1, d)),
        _resident(wqv.shape), _resident(wr.shape), _resident(hm.shape), _resident(kg.shape),
        pl.BlockSpec((2 * HEAD_DIM, tm), lambda i, t: (0, t)),
        pl.BlockSpec((tm, KV_WIDTH), lambda i, t: (t, 0)),
        pl.BlockSpec((tm, KV_WIDTH), lambda i, t: (t, 0)),
        _resident(ab.shape),
    ]
    return pl.pallas_call(
        functools.partial(_in_proj_kernel, tq=tq, tk=tk),
        out_shape=out_shape, grid=(b, nt), in_specs=in_specs, out_specs=out_specs,
        compiler_params=_params("parallel", "parallel"),
        name="in_proj",
    )(x, sh, sc, g, wqv, wr, hm, kg, qtab, kcos, ksin, ab)


def _attn_kernel(*refs, tq, tk, n_ctx, n_lat):
    if n_lat:
        q_ref, kc_ref, vc_ref, k_ref, v_ref, o_ref, m_sc, l_sc, acc_sc = refs
    else:
        q_ref, kc_ref, vc_ref, o_ref, m_sc, l_sc, acc_sc = refs
    qt = q_ref[0, 0, 0]
    m_sc[...] = jnp.full(m_sc.shape, -jnp.inf, F32)
    l_sc[...] = jnp.zeros(l_sc.shape, F32)
    acc_sc[...] = jnp.zeros(acc_sc.shape, F32)

    def step(kblk, vblk):
        s = jnp.dot(kblk, qt, preferred_element_type=F32)
        m_old = m_sc[...]
        m_new = jnp.maximum(m_old, jnp.max(s, axis=0, keepdims=True))
        alpha = jnp.exp(m_old - m_new)
        p = jnp.exp(s - m_new)
        l_sc[...] = alpha * l_sc[...] + jnp.sum(p, axis=0, keepdims=True)
        acc_sc[...] = alpha * acc_sc[...] + jnp.dot(vblk, p.astype(BF16), preferred_element_type=F32)
        m_sc[...] = m_new

    for c in range(n_ctx):
        step(kc_ref[0, c * tk:(c + 1) * tk, :], vc_ref[0, 0, c])
    if n_lat:
        def body(j, carry):
            off = pl.multiple_of(j * tk, tk)
            step(k_ref[0, pl.ds(off, tk), :], v_ref[0, 0, j])
            return carry
        lax.fori_loop(0, n_lat, body, 0)

    o = acc_sc[...] * (1.0 / l_sc[...])
    for r in range(GQA_GROUP):
        o_ref[0, r * HEAD_DIM:(r + 1) * HEAD_DIM, :] = o[:, r * tq:(r + 1) * tq].astype(BF16)


def _attention(qt, kc, vc, k=None, v=None, *, tq, tk):
    b, _, nq, _, n = qt.shape
    length = nq * tq
    n_ctx = kc.shape[1] // tk
    n_lat = 0 if k is None else k.shape[1] // tk
    in_specs = [
        pl.BlockSpec((1, 1, 1, KV_WIDTH, n), lambda i, g, j: (i, g, j, 0, 0)),
        pl.BlockSpec((1,) + kc.shape[1:], lambda i, g, j: (i, 0, 0)),
        pl.BlockSpec((1, 1) + vc.shape[2:], lambda i, g, j: (i, g, 0, 0, 0)),
    ]
    args = [qt, kc, vc]
    if n_lat:
        in_specs += [pl.BlockSpec((1,) + k.shape[1:], lambda i, g, j: (i, 0, 0)),
                     pl.BlockSpec((1, 1) + v.shape[2:], lambda i, g, j: (i, g, 0, 0, 0))]
        args += [k, v]
    return pl.pallas_call(
        functools.partial(_attn_kernel, tq=tq, tk=tk, n_ctx=n_ctx, n_lat=n_lat),
        out_shape=jax.ShapeDtypeStruct((b, Q_WIDTH, length), BF16),
        grid=(b, N_KV_HEADS, nq),
        in_specs=in_specs,
        out_specs=pl.BlockSpec((1, GQA_GROUP * HEAD_DIM, tq), lambda i, g, j: (i, g, j)),
        scratch_shapes=[pltpu.VMEM((1, n), F32), pltpu.VMEM((1, n), F32), pltpu.VMEM((HEAD_DIM, n), F32)],
        compiler_params=_params("parallel", "parallel", "parallel"),
        name="attention",
    )(*args)


def _pool_kernel(u_ref, w_ref, s_ref, o_ref, pad_ref, *, chunk):
    length = u_ref.shape[1]
    width = u_ref.shape[2]
    hw = POOL_HALO
    pad_ref[0:hw, :] = jnp.zeros((hw, width), F32)
    pad_ref[hw:hw + length, :] = u_ref[0]
    pad_ref[hw + length:, :] = jnp.zeros((hw, width), F32)
    lane = lax.broadcasted_iota(jnp.int32, (chunk, width), 1)
    win = jnp.full((chunk, width), POOL_WINDOWS[-1], jnp.int32)
    for gi in range(len(POOL_WINDOWS) - 2, -1, -1):
        win = jnp.where(lane < (gi + 1) * POOL_GROUP, POOL_WINDOWS[gi], win)
    half = win // 2
    for c in range(length // chunk):
        r0 = c * chunk
        sums = {}
        acc = None
        lo, hi = hw, hw
        for w in POOL_WINDOWS:
            for off in list(range(hw - w // 2, lo)) + list(range(hi, hw + w // 2)):
                piece = pad_ref[r0 + off:r0 + off + chunk, :]
                acc = piece if acc is None else acc + piece
            lo, hi = hw - w // 2, hw + w // 2
            sums[w] = acc
        s = sums[POOL_WINDOWS[-1]]
        for gi in range(len(POOL_WINDOWS) - 2, -1, -1):
            s = jnp.where(lane < (gi + 1) * POOL_GROUP, sums[POOL_WINDOWS[gi]], s)
        t = r0 + lax.broadcasted_iota(jnp.int32, (chunk, width), 0)
        cnt = jnp.minimum(t + win - half, length) - jnp.maximum(t - half, 0)
        d = s / cnt.astype(F32) - pad_ref[r0 + hw:r0 + hw + chunk, :]
        y = jnp.dot(d.astype(BF16), w_ref[...], preferred_element_type=F32) * s_ref[...]
        o_ref[0, r0:r0 + chunk, :] = y.astype(BF16)


def _pool(u, w_bd, scale, *, chunk):
    b, length, width = u.shape
    return pl.pallas_call(
        functools.partial(_pool_kernel, chunk=chunk),
        out_shape=jax.ShapeDtypeStruct((b, length, width), BF16),
        grid=(b,),
        in_specs=[pl.BlockSpec((1, length, width), lambda i: (i, 0, 0)),
                  _resident(w_bd.shape), _resident(scale.shape)],
        out_specs=pl.BlockSpec((1, length, width), lambda i: (i, 0, 0)),
        scratch_shapes=[pltpu.VMEM((length + 2 * POOL_HALO, width), F32)],
        compiler_params=_params("parallel"),
        name="pool_mix",
    )(u, w_bd, scale)


def _dft_kernel(d_ref, v_ref, o_ref):
    o_ref[0] = jnp.dot(d_ref[...], v_ref[0], preferred_element_type=F32).astype(BF16)


def _dft(table, v2, *, tl):
    b, two_l, width = v2.shape
    length = two_l // 2
    return pl.pallas_call(
        _dft_kernel,
        out_shape=jax.ShapeDtypeStruct((b, length, width), BF16),
        grid=(length // tl, b),
        in_specs=[pl.BlockSpec((tl, two_l), lambda i, j: (i, 0)),
                  pl.BlockSpec((1, two_l, width), lambda i, j: (j, 0, 0))],
        out_specs=pl.BlockSpec((1, tl, width), lambda i, j: (j, i, 0)),
        compiler_params=_params("parallel", "parallel"),
        name="seq_dft",
    )(table, v2)


def _dft_table(length):
    idx = jnp.arange(length, dtype=jnp.int32)
    ang = ((idx[:, None] * idx[None, :]) % length).astype(F32) * (2.0 * math.pi / length)
    return (jnp.concatenate([jnp.cos(ang), jnp.sin(ang)], axis=1) * (length ** -0.5)).astype(BF16)


def _out_ffn_kernel(x_ref, ot_ref, yp_ref, yf_ref, ga1_ref, sh2_ref, sc2_ref, ga2_ref, g_ref,
                    wo_ref, wgu_ref, wd_ref, o_ref, *, fc):
    d_ff = wd_ref.shape[0]
    mix = lax.dot_general(ot_ref[0], wo_ref[0:Q_WIDTH, :], (((0,), (0,)), ((), ())),
                          preferred_element_type=F32)
    mix += jnp.dot(yp_ref[0], wo_ref[Q_WIDTH:Q_WIDTH + POOL_WIDTH, :], preferred_element_type=F32)
    mix += jnp.dot(yf_ref[0], wo_ref[Q_WIDTH + POOL_WIDTH:, :], preferred_element_type=F32)
    x1 = x_ref[0] + ga1_ref[0] * mix
    f = _rms(x1) * g_ref[...]
    fb = (f * (1.0 + sc2_ref[0]) + sh2_ref[0]).astype(BF16)
    acc = jnp.zeros(x1.shape, F32)
    for c in range(d_ff // fc):
        gate = jnp.dot(fb, wgu_ref[:, c * fc:(c + 1) * fc], preferred_element_type=F32)
        up = jnp.dot(fb, wgu_ref[:, d_ff + c * fc:d_ff + (c + 1) * fc], preferred_element_type=F32)
        hc = (gate / (1.0 + jnp.exp(-gate)) * up).astype(BF16)
        acc += jnp.dot(hc, wd_ref[c * fc:(c + 1) * fc, :], preferred_element_type=F32)
    o_ref[0] = x1 + ga2_ref[0] * acc


def _out_ffn(x, ot, yp, yf, ga1, sh2, sc2, ga2, g, wo, wgu, wd, *, tm, fc):
    b, length, d = x.shape
    bm = ga1.shape[0]
    mod = pl.BlockSpec((1, 1, d), (lambda i, t: (i, 0, 0)) if bm > 1 else (lambda i, t: (0, 0, 0)))
    tok = lambda w: pl.BlockSpec((1, tm, w), lambda i, t: (i, t, 0))
    return pl.pallas_call(
        functools.partial(_out_ffn_kernel, fc=fc),
        out_shape=jax.ShapeDtypeStruct(x.shape, F32),
        grid=(b, length // tm),
        in_specs=[tok(d), pl.BlockSpec((1, Q_WIDTH, tm), lambda i, t: (i, 0, t)),
                  tok(POOL_WIDTH), tok(FOUR_WIDTH), mod, mod, mod, mod, _resident((1, d)),
                  _resident(wo.shape), _resident(wgu.shape), _resident(wd.shape)],
        out_specs=tok(d),
        compiler_params=_params("parallel", "parallel"),
        name="out_ffn",
    )(x, ot, yp, yf, ga1, sh2, sc2, ga2, g, wo, wgu, wd)


def _rope_cos_sin(length, rotate):
    if not rotate:
        return jnp.ones((length, 2, N_FREQ), F32), jnp.zeros((length, 2, N_FREQ), F32)
    rows = length // GRID_W
    row_ids = jnp.repeat(jnp.arange(rows), GRID_W).astype(F32)
    col_ids = jnp.tile(jnp.arange(GRID_W), rows).astype(F32)
    freqs = ROPE_THETA ** (-jnp.arange(N_FREQ, dtype=F32) / N_FREQ)
    ang = jnp.stack([row_ids[:, None] * freqs, col_ids[:, None] * freqs], axis=1)
    return jnp.cos(ang), jnp.sin(ang)


def _q_table(cos, sin, q_gain):
    gq = (q_gain * (HEAD_DIM ** -0.5)).reshape(2, 2, N_FREQ)
    c = jnp.transpose(cos, (1, 2, 0))
    s = jnp.transpose(sin, (1, 2, 0))
    g1, g2 = gq[:, 0, :, None], gq[:, 1, :, None]
    tab = jnp.stack([g1 * c, g2 * s, g2 * c, g1 * s], axis=1)
    return tab.reshape(2 * HEAD_DIM, -1)


def _k_tables(cos, sin):
    c = jnp.concatenate([cos, cos], axis=2).reshape(cos.shape[0], HEAD_DIM)
    s = jnp.concatenate([-sin, sin], axis=2).reshape(sin.shape[0], HEAD_DIM)
    return jnp.tile(c, (1, N_KV_HEADS)), jnp.tile(s, (1, N_KV_HEADS))


def _mix_stream(x, shift, scale, lw, tabs, *, tm, tq, tk, tl, chunk):
    qtab, kcos, ksin, dtab = tabs
    return _in_proj(x, shift, scale, lw["g_mix"], lw["wqv"], lw["wr"], lw["hm"], lw["kg"],
                    qtab, kcos, ksin, lw["ab"], tm=tm, tq=tq, tk=tk)


def kernel(x, c, ctx, c_ctx, w_ada, b_ada, g_mix, g_ffn, w_in, q_gain, k_gain, w_pool, pool_scale,
           w_four, w_out, w_gate_up, w_down):
    b, s, d = x.shape
    cl = ctx.shape[1]
    depth = w_ada.shape[0]
    d_ff = w_down.shape[1]

    tm_x, tm_c = 512, cl
    tq_x, tq_c = 256, cl
    tk = 256
    tl_x, tl_c = 512, cl
    pool_chunk = 256
    fc = 256

    n_rows = -(-(b + 1) // 8) * 8
    cvec = jnp.zeros((n_rows, d), F32).at[:b].set(c).at[b].set(c_ctx)
    mod = _ada(cvec, w_ada, b_ada)
    gr, gi = _four_weights(w_four)

    cos_x, sin_x = _rope_cos_sin(s, True)
    cos_c, sin_c = _rope_cos_sin(cl, False)
    kt_x = _k_tables(cos_x, sin_x)
    kt_c = _k_tables(cos_c, sin_c)
    dtab_x = _dft_table(s)
    dtab_c = _dft_table(cl)
    hm = _block_diag(jnp.full((N_KV_HEADS, HEAD_DIM, HEAD_DIM), 1.0 / HEAD_DIM, F32)).astype(BF16)

    q_end, k_end = Q_WIDTH, Q_WIDTH + KV_WIDTH
    kv_end = k_end + KV_WIDTH
    for i in range(depth):
        update_ctx = i < depth - 1
        wi = w_in[i]
        wqv = jnp.concatenate([wi[:, :q_end], wi[:, k_end:kv_end]], axis=1).T.astype(BF16)
        wr = jnp.concatenate([wi[:, q_end:k_end], wi[:, kv_end:]], axis=1).astype(BF16)
        ab = jnp.concatenate([_block_diag(gr[i]), _block_diag(gi[i])], axis=1).astype(BF16)
        wp = _block_diag(w_pool[i]).astype(BF16)
        ps = pool_scale[i].reshape(1, -1)
        kg = jnp.tile(k_gain[i], N_KV_HEADS).reshape(1, -1)
        wo = w_out[i].astype(BF16)
        wgu = w_gate_up[i].astype(BF16)
        wd = w_down[i].astype(BF16)
        gm = g_mix[i].reshape(1, d)
        gf = g_ffn[i].reshape(1, d)
        mx = [mod[i, :b, j * d:(j + 1) * d].reshape(b, 1, d) for j in range(6)]
        mc = [mod[i, b, j * d:(j + 1) * d].reshape(1, 1, d) for j in range(6)]

        def in_proj(xx, m, cos, sin, kt, tm, tq):
            return _in_proj(xx, m[0], m[1], gm, wqv, wr, hm, kg, _q_table(cos, sin, q_gain[i]),
                            kt[0], kt[1], ab, tm=tm, tq=tq, tk=tk)

        def finish(xx, ot, up, vf, m, dtab, tm, tl):
            yp = _pool(up, wp, ps, chunk=pool_chunk)
            yf = _dft(dtab, vf.reshape(vf.shape[0], -1, FOUR_WIDTH), tl=tl)
            return _out_ffn(xx, ot, yp, yf, m[2], m[3], m[4], m[5], gf, wo, wgu, wd, tm=tm, fc=fc)

        qt_c, k_c, vt_c, up_c, vf_c = in_proj(ctx, mc, cos_c, sin_c, kt_c, tm_c, tq_c)
        qt_x, k_x, vt_x, up_x, vf_x = in_proj(x, mx, cos_x, sin_x, kt_x, tm_x, tq_x)
        ot_x = _attention(qt_x, k_c, vt_c, k_x, vt_x, tq=tq_x, tk=tk)
        x = finish(x, ot_x, up_x, vf_x, mx, dtab_x, tm_x, tl_x)
        if update_ctx:
            ot_c = _attention(qt_c, k_c, vt_c, tq=tq_c, tk=tk)
            ctx = finish(ctx, ot_c, up_c, vf_c, mc, dtab_c, tm_c, tl_c)
    return x
```

```python
import functools
import math

import jax
import jax.numpy as jnp
from jax import lax
from jax.experimental import pallas as pl
from jax.experimental.pallas import tpu as pltpu

F32 = jnp.float32
BF16 = jnp.bfloat16

GRID_W = 64
HEAD_DIM = 64
N_Q_HEADS = 8
N_KV_HEADS = 2
GQA_GROUP = N_Q_HEADS // N_KV_HEADS
Q_WIDTH = N_Q_HEADS * HEAD_DIM
KV_WIDTH = N_KV_HEADS * HEAD_DIM
POOL_WINDOWS = (2, 4, 8, 16)
POOL_WIDTH = 256
POOL_GROUP = 64
FOUR_WIDTH = 256
FOUR_GROUP = 64
N_FREQ = HEAD_DIM // 4
ROPE_THETA = 10000.0
EPS = 1e-6
POOL_HALO = max(POOL_WINDOWS) // 2
ATTN_LOOKAHEAD = 8
ATTN_UNROLL = 32

VMEM_LIMIT_BYTES = 56 * 1024 * 1024


def _params(*sem):
    return pltpu.CompilerParams(dimension_semantics=sem, vmem_limit_bytes=VMEM_LIMIT_BYTES)


def _resident(shape):
    zeros = (0,) * len(shape)
    return pl.BlockSpec(shape, lambda *_: zeros, pipeline_mode=pl.Buffered(1))


def _rms(x):
    return x * lax.rsqrt(jnp.mean(x * x, axis=-1, keepdims=True) + EPS)


def _ada_kernel(c_ref, w_ref, b_ref, o_ref):
    c = c_ref[...]
    sc = c / (1.0 + jnp.exp(-c))
    o_ref[0] = jnp.dot(sc, w_ref[0], preferred_element_type=F32,
                       precision=lax.Precision.HIGHEST) + b_ref[0]


def _ada(cvec, w_ada, b_ada):
    depth, d, n = w_ada.shape
    r = cvec.shape[0]
    tn = 1536
    return pl.pallas_call(
        _ada_kernel,
        out_shape=jax.ShapeDtypeStruct((depth, r, n), F32),
        grid=(depth, n // tn),
        in_specs=[pl.BlockSpec((r, d), lambda i, j: (0, 0)),
                  pl.BlockSpec((1, d, tn), lambda i, j: (i, 0, j)),
                  pl.BlockSpec((1, 1, tn), lambda i, j: (i, 0, j))],
        out_specs=pl.BlockSpec((1, r, tn), lambda i, j: (i, 0, j)),
        compiler_params=_params("arbitrary", "arbitrary"),
        name="ada_mod",
    )(cvec, w_ada, b_ada.reshape(depth, 1, n))


def _four_w_kernel(c_ref, s_ref, w_ref, gr_ref, gi_ref):
    for g in range(w_ref.shape[1]):
        w = w_ref[0, g]
        gr_ref[0, g] = jnp.dot(c_ref[...], w, preferred_element_type=F32, precision=lax.Precision.HIGHEST)
        gi_ref[0, g] = -jnp.dot(s_ref[...], w, preferred_element_type=F32, precision=lax.Precision.HIGHEST)


def _four_weights(w_four):
    depth, ng, gc, _ = w_four.shape
    c = jnp.arange(gc, dtype=jnp.int32)
    ang = ((c[:, None] * c[None, :]) % gc).astype(F32) * (2.0 * math.pi / gc)
    scale = gc ** -0.5
    spec = pl.BlockSpec((1, ng, gc, gc), lambda i: (i, 0, 0, 0))
    tab = pl.BlockSpec((gc, gc), lambda i: (0, 0))
    gr, gi = pl.pallas_call(
        _four_w_kernel,
        out_shape=[jax.ShapeDtypeStruct(w_four.shape, F32)] * 2,
        grid=(depth,),
        in_specs=[tab, tab, spec],
        out_specs=[spec, spec],
        compiler_params=_params("arbitrary"),
        name="four_weights",
    )(jnp.cos(ang) * scale, jnp.sin(ang) * scale, w_four)
    return gr, gi


def _block_diag(blocks):
    g, a, b = blocks.shape
    eye = jnp.eye(g, dtype=blocks.dtype)
    return (eye[:, None, :, None] * blocks[:, :, None, :]).reshape(g * a, g * b)


def _in_proj_kernel(x_ref, sh_ref, sc_ref, g_ref, wqv_ref, wr_ref, hm_ref, kg_ref, qtab_ref,
                    kcos_ref, ksin_ref, ab_ref, q_ref, k_ref, v_ref, up_ref, vf_ref, *, tq, tk):
    tm = x_ref.shape[1]
    h = _rms(x_ref[0]) * g_ref[...]
    h = h * (1.0 + sc_ref[0]) + sh_ref[0]
    hb = h.astype(BF16)
    zt = lax.dot_general(wqv_ref[...], hb, (((1,), (1,)), ((), ())), preferred_element_type=F32)
    zr = jnp.dot(hb, wr_ref[...], preferred_element_type=F32)

    tab = qtab_ref[...]
    for hd in range(N_Q_HEADS):
        qh = zt[hd * HEAD_DIM:(hd + 1) * HEAD_DIM, :]
        ss = jnp.sum(qh * qh, axis=0, keepdims=True) * (1.0 / HEAD_DIM)
        qn = qh * lax.rsqrt(ss + EPS)
        parts = []
        for seg in range(2):
            x1 = qn[seg * 32:seg * 32 + N_FREQ]
            x2 = qn[seg * 32 + N_FREQ:seg * 32 + 2 * N_FREQ]
            t0 = seg * 64
            g1c, g2s = tab[t0:t0 + 16], tab[t0 + 16:t0 + 32]
            g2c, g1s = tab[t0 + 32:t0 + 48], tab[t0 + 48:t0 + 64]
            parts += [x1 * g1c - x2 * g2s, x2 * g2c + x1 * g1s]
        qr = jnp.concatenate(parts, axis=0).astype(BF16)
        g = hd // GQA_GROUP
        o = 1 - g
        for j in range(tm // tq):
            q_ref[0, j, hd, g * HEAD_DIM:(g + 1) * HEAD_DIM, :] = qr[:, j * tq:(j + 1) * tq]
            q_ref[0, j, hd, o * HEAD_DIM:(o + 1) * HEAD_DIM, :] = jnp.zeros((HEAD_DIM, tq), BF16)

    vt = zt[Q_WIDTH:Q_WIDTH + KV_WIDTH].astype(BF16)
    for g in range(N_KV_HEADS):
        for j in range(tm // tk):
            v_ref[0, g, j] = vt[g * HEAD_DIM:(g + 1) * HEAD_DIM, j * tk:(j + 1) * tk]

    kz = zr[:, :KV_WIDTH]
    ssk = jnp.dot((kz * kz).astype(BF16), hm_ref[...], preferred_element_type=F32)
    kn = kz * lax.rsqrt(ssk + EPS) * kg_ref[...]
    lane = lax.broadcasted_iota(jnp.int32, kn.shape, 1)
    rot = jnp.where((lane % 32) < N_FREQ,
                    pltpu.roll(kn, KV_WIDTH - N_FREQ, 1), pltpu.roll(kn, N_FREQ, 1))
    k_ref[0] = (kn * kcos_ref[...] + rot * ksin_ref[...]).astype(BF16)

    up_ref[0] = zr[:, KV_WIDTH:KV_WIDTH + POOL_WIDTH]
    uf = zr[:, KV_WIDTH + POOL_WIDTH:].astype(BF16)
    v2 = jnp.dot(uf, ab_ref[...], preferred_element_type=F32)
    vf_ref[0, 0] = v2[:, :FOUR_WIDTH].astype(BF16)
    vf_ref[0, 1] = v2[:, FOUR_WIDTH:].astype(BF16)


def _in_proj(x, sh, sc, g, wqv, wr, hm, kg, qtab, kcos, ksin, ab, *, tm, tq, tk):
    b, length, d = x.shape
    nt = length // tm
    bm = sh.shape[0]
    mod = pl.BlockSpec((1, 1, d), (lambda i, t: (i, 0, 0)) if bm > 1 else (lambda i, t: (0, 0, 0)))
    out_shape = [
        jax.ShapeDtypeStruct((b, length // tq, N_Q_HEADS, KV_WIDTH, tq), BF16),
        jax.ShapeDtypeStruct((b, length, KV_WIDTH), BF16),
        jax.ShapeDtypeStruct((b, N_KV_HEADS, length // tk, HEAD_DIM, tk), BF16),
        jax.ShapeDtypeStruct((b, length, POOL_WIDTH), F32),
        jax.ShapeDtypeStruct((b, 2, length, FOUR_WIDTH), BF16),
    ]
    out_specs = [
        pl.BlockSpec((1, tm // tq, N_Q_HEADS, KV_WIDTH, tq), lambda i, t: (i, t, 0, 0, 0)),
        pl.BlockSpec((1, tm, KV_WIDTH), lambda i, t: (i, t, 0)),
        pl.BlockSpec((1, N_KV_HEADS, tm // tk, HEAD_DIM, tk), lambda i, t: (i, 0, t, 0, 0)),
        pl.BlockSpec((1, tm, POOL_WIDTH), lambda i, t: (i, t, 0)),
        pl.BlockSpec((1, 2, tm, FOUR_WIDTH), lambda i, t: (i, 0, t, 0)),
    ]
    in_specs = [
        pl.BlockSpec((1, tm, d), lambda i, t: (i, t, 0)),
        mod, mod,
        _resident((1, d)),
        _resident(wqv.shape), _resident(wr.shape), _resident(hm.shape), _resident(kg.shape),
        pl.BlockSpec((2 * HEAD_DIM, tm), lambda i, t: (0, t)),
        pl.BlockSpec((tm, KV_WIDTH), lambda i, t: (t, 0)),
        pl.BlockSpec((tm, KV_WIDTH), lambda i, t: (t, 0)),
        _resident(ab.shape),
    ]
    return pl.pallas_call(
        functools.partial(_in_proj_kernel, tq=tq, tk=tk),
        out_shape=out_shape, grid=(b, nt), in_specs=in_specs, out_specs=out_specs,
        compiler_params=_params("parallel", "parallel"),
        name="in_proj",
    )(x, sh, sc, g, wqv, wr, hm, kg, qtab, kcos, ksin, ab)


def _attn_kernel(*refs, tk, n_ctx, n_lat, ahead, unroll):
    if n_lat:
        q_ref, kc_ref, vc_ref, k_ref, v_ref, o_ref, m_sc, l_sc, acc_sc = refs
    else:
        q_ref, kc_ref, vc_ref, o_ref, m_sc, l_sc, acc_sc = refs
    m_sc[...] = jnp.full(m_sc.shape, -jnp.inf, F32)
    l_sc[...] = jnp.zeros(l_sc.shape, F32)
    acc_sc[...] = jnp.zeros(acc_sc.shape, F32)

    def step(blocks):
        items = [(c, hd) for c in range(len(blocks)) for hd in range(N_Q_HEADS)]
        scores = {}

        def issue(i):
            c, hd = items[i]
            scores[i] = jnp.dot(blocks[c][0], q_ref[0, 0, hd], preferred_element_type=F32)

        for i in range(min(ahead, len(items))):
            issue(i)
        for i, (c, hd) in enumerate(items):
            if i + ahead < len(items):
                issue(i + ahead)
            s = scores.pop(i)
            m_old = m_sc[hd]
            m_new = jnp.maximum(m_old, jnp.max(s, axis=0, keepdims=True))
            alpha = jnp.exp2(m_old - m_new)
            p = jnp.exp2(s - m_new)
            l_sc[hd] = alpha * l_sc[hd] + jnp.sum(p, axis=0, keepdims=True)
            acc_sc[hd] = alpha * acc_sc[hd] + jnp.dot(blocks[c][1][hd // GQA_GROUP], p.astype(BF16),
                                                      preferred_element_type=F32)
            m_sc[hd] = m_new

    def lat_blocks(first):
        blocks = []
        for u in range(unroll):
            idx = first + u
            row = idx * tk if isinstance(idx, int) else pl.multiple_of(idx * tk, tk)
            blocks.append((k_ref[0, pl.ds(row, tk), :], [v_ref[0, g, idx] for g in range(N_KV_HEADS)]))
        return blocks

    ctx_blocks = [(kc_ref[0, c * tk:(c + 1) * tk, :], [vc_ref[0, g, c] for g in range(N_KV_HEADS)])
                  for c in range(n_ctx)]
    if n_lat and unroll == n_lat:
        step(ctx_blocks + lat_blocks(0))
    else:
        step(ctx_blocks)
        if n_lat:
            def body(j, carry):
                step(lat_blocks(j * unroll))
                return carry
            lax.fori_loop(0, n_lat // unroll, body, 0)

    for hd in range(N_Q_HEADS):
        o = acc_sc[hd] * (1.0 / l_sc[hd])
        o_ref[0, hd * HEAD_DIM:(hd + 1) * HEAD_DIM, :] = o.astype(BF16)


def _attention(qt, kc, vc, k=None, v=None, *, tk):
    b, nq, _, _, tq = qt.shape
    length = nq * tq
    n_ctx = kc.shape[1] // tk
    n_lat = 0 if k is None else k.shape[1] // tk
    in_specs = [
        pl.BlockSpec((1, 1) + qt.shape[2:], lambda i, j: (i, j, 0, 0, 0)),
        pl.BlockSpec((1,) + kc.shape[1:], lambda i, j: (i, 0, 0)),
        pl.BlockSpec((1,) + vc.shape[1:], lambda i, j: (i, 0, 0, 0, 0)),
    ]
    args = [qt, kc, vc]
    if n_lat:
        in_specs += [pl.BlockSpec((1,) + k.shape[1:], lambda i, j: (i, 0, 0)),
                     pl.BlockSpec((1,) + v.shape[1:], lambda i, j: (i, 0, 0, 0, 0))]
        args += [k, v]
    return pl.pallas_call(
        functools.partial(_attn_kernel, tk=tk, n_ctx=n_ctx, n_lat=n_lat, ahead=ATTN_LOOKAHEAD,
                          unroll=math.gcd(ATTN_UNROLL, max(n_lat, 1))),
        out_shape=jax.ShapeDtypeStruct((b, Q_WIDTH, length), BF16),
        grid=(b, nq),
        in_specs=in_specs,
        out_specs=pl.BlockSpec((1, Q_WIDTH, tq), lambda i, j: (i, 0, j)),
        scratch_shapes=[pltpu.VMEM((N_Q_HEADS, 1, tq), F32), pltpu.VMEM((N_Q_HEADS, 1, tq), F32),
                        pltpu.VMEM((N_Q_HEADS, HEAD_DIM, tq), F32)],
        compiler_params=_params("parallel", "parallel"),
        name="attention",
    )(*args)


def _pool_kernel(u_ref, w_ref, s_ref, o_ref, pad_ref, *, chunk):
    length = u_ref.shape[1]
    width = u_ref.shape[2]
    hw = POOL_HALO
    pad_ref[0:hw, :] = jnp.zeros((hw, width), F32)
    pad_ref[hw:hw + length, :] = u_ref[0]
    pad_ref[hw + length:, :] = jnp.zeros((hw, width), F32)
    lane = lax.broadcasted_iota(jnp.int32, (chunk, width), 1)
    win = jnp.full((chunk, width), POOL_WINDOWS[-1], jnp.int32)
    for gi in range(len(POOL_WINDOWS) - 2, -1, -1):
        win = jnp.where(lane < (gi + 1) * POOL_GROUP, POOL_WINDOWS[gi], win)
    half = win // 2
    for c in range(length // chunk):
        r0 = c * chunk
        sums = {}
        acc = None
        lo, hi = hw, hw
        for w in POOL_WINDOWS:
            for off in list(range(hw - w // 2, lo)) + list(range(hi, hw + w // 2)):
                piece = pad_ref[r0 + off:r0 + off + chunk, :]
                acc = piece if acc is None else acc + piece
            lo, hi = hw - w // 2, hw + w // 2
            sums[w] = acc
        s = sums[POOL_WINDOWS[-1]]
        for gi in range(len(POOL_WINDOWS) - 2, -1, -1):
            s = jnp.where(lane < (gi + 1) * POOL_GROUP, sums[POOL_WINDOWS[gi]], s)
        t = r0 + lax.broadcasted_iota(jnp.int32, (chunk, width), 0)
        cnt = jnp.minimum(t + win - half, length) - jnp.maximum(t - half, 0)
        d = s / cnt.astype(F32) - pad_ref[r0 + hw:r0 + hw + chunk, :]
        y = jnp.dot(d.astype(BF16), w_ref[...], preferred_element_type=F32) * s_ref[...]
        o_ref[0, r0:r0 + chunk, :] = y.astype(BF16)


def _pool(u, w_bd, scale, *, chunk):
    b, length, width = u.shape
    return pl.pallas_call(
        functools.partial(_pool_kernel, chunk=chunk),
        out_shape=jax.ShapeDtypeStruct((b, length, width), BF16),
        grid=(b,),
        in_specs=[pl.BlockSpec((1, length, width), lambda i: (i, 0, 0)),
                  _resident(w_bd.shape), _resident(scale.shape)],
        out_specs=pl.BlockSpec((1, length, width), lambda i: (i, 0, 0)),
        scratch_shapes=[pltpu.VMEM((length + 2 * POOL_HALO, width), F32)],
        compiler_params=_params("parallel"),
        name="pool_mix",
    )(u, w_bd, scale)


def _dft_kernel(d_ref, v_ref, o_ref):
    o_ref[0] = jnp.dot(d_ref[...], v_ref[0], preferred_element_type=F32).astype(BF16)


def _dft(table, v2, *, tl):
    b, two_l, width = v2.shape
    length = two_l // 2
    return pl.pallas_call(
        _dft_kernel,
        out_shape=jax.ShapeDtypeStruct((b, length, width), BF16),
        grid=(length // tl, b),
        in_specs=[pl.BlockSpec((tl, two_l), lambda i, j: (i, 0)),
                  pl.BlockSpec((1, two_l, width), lambda i, j: (j, 0, 0))],
        out_specs=pl.BlockSpec((1, tl, width), lambda i, j: (j, i, 0)),
        compiler_params=_params("parallel", "parallel"),
        name="seq_dft",
    )(table, v2)


def _dft_table(length):
    idx = jnp.arange(length, dtype=jnp.int32)
    ang = ((idx[:, None] * idx[None, :]) % length).astype(F32) * (2.0 * math.pi / length)
    return (jnp.concatenate([jnp.cos(ang), jnp.sin(ang)], axis=1) * (length ** -0.5)).astype(BF16)


def _out_ffn_kernel(x_ref, ot_ref, yp_ref, yf_ref, ga1_ref, sh2_ref, sc2_ref, ga2_ref, g_ref,
                    wo_ref, wgu_ref, wd_ref, o_ref, *, fc):
    d_ff = wd_ref.shape[0]
    mix = lax.dot_general(ot_ref[0], wo_ref[0:Q_WIDTH, :], (((0,), (0,)), ((), ())),
                          preferred_element_type=F32)
    mix += jnp.dot(yp_ref[0], wo_ref[Q_WIDTH:Q_WIDTH + POOL_WIDTH, :], preferred_element_type=F32)
    mix += jnp.dot(yf_ref[0], wo_ref[Q_WIDTH + POOL_WIDTH:, :], preferred_element_type=F32)
    x1 = x_ref[0] + ga1_ref[0] * mix
    f = _rms(x1) * g_ref[...]
    fb = (f * (1.0 + sc2_ref[0]) + sh2_ref[0]).astype(BF16)
    acc = jnp.zeros(x1.shape, F32)
    for c in range(d_ff // fc):
        gate = jnp.dot(fb, wgu_ref[:, c * fc:(c + 1) * fc], preferred_element_type=F32)
        up = jnp.dot(fb, wgu_ref[:, d_ff + c * fc:d_ff + (c + 1) * fc], preferred_element_type=F32)
        hc = (gate / (1.0 + jnp.exp(-gate)) * up).astype(BF16)
        acc += jnp.dot(hc, wd_ref[c * fc:(c + 1) * fc, :], preferred_element_type=F32)
    o_ref[0] = x1 + ga2_ref[0] * acc


def _out_ffn(x, ot, yp, yf, ga1, sh2, sc2, ga2, g, wo, wgu, wd, *, tm, fc):
    b, length, d = x.shape
    bm = ga1.shape[0]
    mod = pl.BlockSpec((1, 1, d), (lambda i, t: (i, 0, 0)) if bm > 1 else (lambda i, t: (0, 0, 0)))
    tok = lambda w: pl.BlockSpec((1, tm, w), lambda i, t: (i, t, 0))
    return pl.pallas_call(
        functools.partial(_out_ffn_kernel, fc=fc),
        out_shape=jax.ShapeDtypeStruct(x.shape, F32),
        grid=(b, length // tm),
        in_specs=[tok(d), pl.BlockSpec((1, Q_WIDTH, tm), lambda i, t: (i, 0, t)),
                  tok(POOL_WIDTH), tok(FOUR_WIDTH), mod, mod, mod, mod, _resident((1, d)),
                  _resident(wo.shape), _resident(wgu.shape), _resident(wd.shape)],
        out_specs=tok(d),
        compiler_params=_params("parallel", "parallel"),
        name="out_ffn",
    )(x, ot, yp, yf, ga1, sh2, sc2, ga2, g, wo, wgu, wd)


def _rope_cos_sin(length, rotate):
    if not rotate:
        return jnp.ones((length, 2, N_FREQ), F32), jnp.zeros((length, 2, N_FREQ), F32)
    rows = length // GRID_W
    row_ids = jnp.repeat(jnp.arange(rows), GRID_W).astype(F32)
    col_ids = jnp.tile(jnp.arange(GRID_W), rows).astype(F32)
    freqs = ROPE_THETA ** (-jnp.arange(N_FREQ, dtype=F32) / N_FREQ)
    ang = jnp.stack([row_ids[:, None] * freqs, col_ids[:, None] * freqs], axis=1)
    return jnp.cos(ang), jnp.sin(ang)


def _q_table(cos, sin, q_gain):
    gq = (q_gain * (HEAD_DIM ** -0.5 * math.log2(math.e))).reshape(2, 2, N_FREQ)
    c = jnp.transpose(cos, (1, 2, 0))
    s = jnp.transpose(sin, (1, 2, 0))
    g1, g2 = gq[:, 0, :, None], gq[:, 1, :, None]
    tab = jnp.stack([g1 * c, g2 * s, g2 * c, g1 * s], axis=1)
    return tab.reshape(2 * HEAD_DIM, -1)


def _k_tables(cos, sin):
    c = jnp.concatenate([cos, cos], axis=2).reshape(cos.shape[0], HEAD_DIM)
    s = jnp.concatenate([-sin, sin], axis=2).reshape(sin.shape[0], HEAD_DIM)
    return jnp.tile(c, (1, N_KV_HEADS)), jnp.tile(s, (1, N_KV_HEADS))


def kernel(x, c, ctx, c_ctx, w_ada, b_ada, g_mix, g_ffn, w_in, q_gain, k_gain, w_pool, pool_scale,
           w_four, w_out, w_gate_up, w_down):
    b, s, d = x.shape
    cl = ctx.shape[1]
    depth = w_ada.shape[0]
    d_ff = w_down.shape[1]

    tm_x, tm_c = 512, cl
    tq_x, tq_c = 256, cl
    tk = 128
    tl_x, tl_c = 512, cl
    pool_chunk = 256
    fc = 256

    n_rows = -(-(b + 1) // 8) * 8
    cvec = jnp.zeros((n_rows, d), F32).at[:b].set(c).at[b].set(c_ctx)
    mod = _ada(cvec, w_ada, b_ada)
    gr, gi = _four_weights(w_four)

    cos_x, sin_x = _rope_cos_sin(s, True)
    cos_c, sin_c = _rope_cos_sin(cl, False)
    kt_x = _k_tables(cos_x, sin_x)
    kt_c = _k_tables(cos_c, sin_c)
    dtab_x = _dft_table(s)
    dtab_c = _dft_table(cl)
    hm = _block_diag(jnp.full((N_KV_HEADS, HEAD_DIM, HEAD_DIM), 1.0 / HEAD_DIM, F32)).astype(BF16)

    q_end, k_end = Q_WIDTH, Q_WIDTH + KV_WIDTH
    kv_end = k_end + KV_WIDTH
    for i in range(depth):
        update_ctx = i < depth - 1
        wi = w_in[i]
        wqv = jnp.concatenate([wi[:, :q_end], wi[:, k_end:kv_end]], axis=1).T.astype(BF16)
        wr = jnp.concatenate([wi[:, q_end:k_end], wi[:, kv_end:]], axis=1).astype(BF16)
        ab = jnp.concatenate([_block_diag(gr[i]), _block_diag(gi[i])], axis=1).astype(BF16)
        wp = _block_diag(w_pool[i]).astype(BF16)
        ps = pool_scale[i].reshape(1, -1)
        kg = jnp.tile(k_gain[i], N_KV_HEADS).reshape(1, -1)
        wo = w_out[i].astype(BF16)
        wgu = w_gate_up[i].astype(BF16)
        wd = w_down[i].astype(BF16)
        gm = g_mix[i].reshape(1, d)
        gf = g_ffn[i].reshape(1, d)
        mx = [mod[i, :b, j * d:(j + 1) * d].reshape(b, 1, d) for j in range(6)]
        mc = [mod[i, b, j * d:(j + 1) * d].reshape(1, 1, d) for j in range(6)]

        def in_proj(xx, m, cos, sin, kt, tm, tq):
            return _in_proj(xx, m[0], m[1], gm, wqv, wr, hm, kg, _q_table(cos, sin, q_gain[i]),
                            kt[0], kt[1], ab, tm=tm, tq=tq, tk=tk)

        def finish(xx, ot, up, vf, m, dtab, tm, tl):
            yp = _pool(up, wp, ps, chunk=pool_chunk)
            yf = _dft(dtab, vf.reshape(vf.shape[0], -1, FOUR_WIDTH), tl=tl)
            return _out_ffn(xx, ot, yp, yf, m[2], m[3], m[4], m[5], gf, wo, wgu, wd, tm=tm, fc=fc)

        qt_c, k_c, vt_c, up_c, vf_c = in_proj(ctx, mc, cos_c, sin_c, kt_c, tm_c, tq_c)
        qt_x, k_x, vt_x, up_x, vf_x = in_proj(x, mx, cos_x, sin_x, kt_x, tm_x, tq_x)
        ot_x = _attention(qt_x, k_c, vt_c, k_x, vt_x, tk=tk)
        x = finish(x, ot_x, up_x, vf_x, mx, dtab_x, tm_x, tl_x)
        if update_ctx:
            ot_c = _attention(qt_c, k_c, vt_c, tk=tk)
            ctx = finish(ctx, ot_c, up_c, vf_c, mc, dtab_c, tm_c, tl_c)
    return x
```

```python
import functools
import math

import jax
import jax.numpy as jnp
from jax import lax
from jax.experimental import pallas as pl
from jax.experimental.pallas import tpu as pltpu

F32 = jnp.float32
BF16 = jnp.bfloat16

GRID_W = 64
HEAD_DIM = 64
N_Q_HEADS = 8
N_KV_HEADS = 2
GQA_GROUP = N_Q_HEADS // N_KV_HEADS
Q_WIDTH = N_Q_HEADS * HEAD_DIM
KV_WIDTH = N_KV_HEADS * HEAD_DIM
POOL_WINDOWS = (2, 4, 8, 16)
POOL_WIDTH = 256
POOL_GROUP = 64
FOUR_WIDTH = 256
FOUR_GROUP = 64
N_FREQ = HEAD_DIM // 4
ROPE_THETA = 10000.0
EPS = 1e-6
POOL_HALO = max(POOL_WINDOWS) // 2
LANES = 128
V_ROWS = HEAD_DIM + 16
ATTN_LOOKAHEAD = 8
ATTN_UNROLL = 32

VMEM_LIMIT_BYTES = 56 * 1024 * 1024


def _params(*sem):
    return pltpu.CompilerParams(dimension_semantics=sem, vmem_limit_bytes=VMEM_LIMIT_BYTES)


def _resident(shape):
    zeros = (0,) * len(shape)
    return pl.BlockSpec(shape, lambda *_: zeros, pipeline_mode=pl.Buffered(1))


def _rms(x):
    return x * lax.rsqrt(jnp.mean(x * x, axis=-1, keepdims=True) + EPS)


def _ada_kernel(c_ref, w_ref, b_ref, o_ref):
    c = c_ref[...]
    sc = c / (1.0 + jnp.exp(-c))
    o_ref[0] = jnp.dot(sc, w_ref[0], preferred_element_type=F32,
                       precision=lax.Precision.HIGHEST) + b_ref[0]


def _ada(cvec, w_ada, b_ada):
    depth, d, n = w_ada.shape
    r = cvec.shape[0]
    tn = 1536
    return pl.pallas_call(
        _ada_kernel,
        out_shape=jax.ShapeDtypeStruct((depth, r, n), F32),
        grid=(depth, n // tn),
        in_specs=[pl.BlockSpec((r, d), lambda i, j: (0, 0)),
                  pl.BlockSpec((1, d, tn), lambda i, j: (i, 0, j)),
                  pl.BlockSpec((1, 1, tn), lambda i, j: (i, 0, j))],
        out_specs=pl.BlockSpec((1, r, tn), lambda i, j: (i, 0, j)),
        compiler_params=_params("arbitrary", "arbitrary"),
        name="ada_mod",
    )(cvec, w_ada, b_ada.reshape(depth, 1, n))


def _four_w_kernel(c_ref, s_ref, w_ref, gr_ref, gi_ref):
    for g in range(w_ref.shape[1]):
        w = w_ref[0, g]
        gr_ref[0, g] = jnp.dot(c_ref[...], w, preferred_element_type=F32, precision=lax.Precision.HIGHEST)
        gi_ref[0, g] = -jnp.dot(s_ref[...], w, preferred_element_type=F32, precision=lax.Precision.HIGHEST)


def _four_weights(w_four):
    depth, ng, gc, _ = w_four.shape
    c = jnp.arange(gc, dtype=jnp.int32)
    ang = ((c[:, None] * c[None, :]) % gc).astype(F32) * (2.0 * math.pi / gc)
    scale = gc ** -0.5
    spec = pl.BlockSpec((1, ng, gc, gc), lambda i: (i, 0, 0, 0))
    tab = pl.BlockSpec((gc, gc), lambda i: (0, 0))
    gr, gi = pl.pallas_call(
        _four_w_kernel,
        out_shape=[jax.ShapeDtypeStruct(w_four.shape, F32)] * 2,
        grid=(depth,),
        in_specs=[tab, tab, spec],
        out_specs=[spec, spec],
        compiler_params=_params("arbitrary"),
        name="four_weights",
    )(jnp.cos(ang) * scale, jnp.sin(ang) * scale, w_four)
    return gr, gi


def _block_diag(blocks):
    g, a, b = blocks.shape
    eye = jnp.eye(g, dtype=blocks.dtype)
    return (eye[:, None, :, None] * blocks[:, :, None, :]).reshape(g * a, g * b)


def _in_proj_kernel(x_ref, sh_ref, sc_ref, g_ref, wqv_ref, wr_ref, hm_ref, kg_ref, qtab_ref,
                    kcos_ref, ksin_ref, ab_ref, q_ref, k_ref, v_ref, up_ref, vf_ref, vf_sc, *, tq, tk):
    tm = x_ref.shape[1]
    h = _rms(x_ref[0]) * g_ref[...]
    h = h * (1.0 + sc_ref[0]) + sh_ref[0]
    hb = h.astype(BF16)
    zt = lax.dot_general(wqv_ref[...], hb, (((1,), (1,)), ((), ())), preferred_element_type=F32)
    zr = jnp.dot(hb, wr_ref[...], preferred_element_type=F32)

    tab = qtab_ref[...]
    for hd in range(N_Q_HEADS):
        qh = zt[hd * HEAD_DIM:(hd + 1) * HEAD_DIM, :]
        ss = jnp.sum(qh * qh, axis=0, keepdims=True) * (1.0 / HEAD_DIM)
        qn = qh * lax.rsqrt(ss + EPS)
        parts = []
        for seg in range(2):
            x1 = qn[seg * 32:seg * 32 + N_FREQ]
            x2 = qn[seg * 32 + N_FREQ:seg * 32 + 2 * N_FREQ]
            t0 = seg * 64
            g1c, g2s = tab[t0:t0 + 16], tab[t0 + 16:t0 + 32]
            g2c, g1s = tab[t0 + 32:t0 + 48], tab[t0 + 48:t0 + 64]
            parts += [x1 * g1c - x2 * g2s, x2 * g2c + x1 * g1s]
        qr = jnp.concatenate(parts, axis=0).astype(BF16)
        g = hd // GQA_GROUP
        o = 1 - g
        for j in range(tm // tq):
            q_ref[0, j, hd, g * HEAD_DIM:(g + 1) * HEAD_DIM, :] = qr[:, j * tq:(j + 1) * tq]
            q_ref[0, j, hd, o * HEAD_DIM:(o + 1) * HEAD_DIM, :] = jnp.zeros((HEAD_DIM, tq), BF16)

    vt = zt[Q_WIDTH:Q_WIDTH + KV_WIDTH].astype(BF16)
    for g in range(N_KV_HEADS):
        for j in range(tm // tk):
            v_ref[0, g, j, 0:HEAD_DIM, :] = vt[g * HEAD_DIM:(g + 1) * HEAD_DIM, j * tk:(j + 1) * tk]
            v_ref[0, g, j, HEAD_DIM:, :] = jnp.ones((V_ROWS - HEAD_DIM, tk), BF16)

    kz = zr[:, :KV_WIDTH]
    ssk = jnp.dot((kz * kz).astype(BF16), hm_ref[...], preferred_element_type=F32)
    kn = kz * lax.rsqrt(ssk + EPS) * kg_ref[...]
    lane = lax.broadcasted_iota(jnp.int32, kn.shape, 1)
    rot = jnp.where((lane % 32) < N_FREQ,
                    pltpu.roll(kn, KV_WIDTH - N_FREQ, 1), pltpu.roll(kn, N_FREQ, 1))
    k_ref[0] = (kn * kcos_ref[...] + rot * ksin_ref[...]).astype(BF16)

    up_ref[0] = zr[:, KV_WIDTH:KV_WIDTH + POOL_WIDTH]
    uf = zr[:, KV_WIDTH + POOL_WIDTH:].astype(BF16)
    v2 = jnp.dot(uf, ab_ref[...], preferred_element_type=F32)
    n_chunks = 2 * FOUR_WIDTH // LANES
    per_part = FOUR_WIDTH // LANES
    for ch in range(n_chunks):
        vf_sc[ch] = v2[:, ch * LANES:(ch + 1) * LANES]
    for par in range(2):
        for ch in range(n_chunks):
            rows = vf_sc[ch, pl.ds(par, tm // 2, stride=2), :]
            part, col = divmod(ch, per_part)
            vf_ref[0, par, part, :, col * LANES:(col + 1) * LANES] = rows.astype(BF16)


def _in_proj(x, sh, sc, g, wqv, wr, hm, kg, qtab, kcos, ksin, ab, *, tm, tq, tk):
    b, length, d = x.shape
    nt = length // tm
    bm = sh.shape[0]
    mod = pl.BlockSpec((1, 1, d), (lambda i, t: (i, 0, 0)) if bm > 1 else (lambda i, t: (0, 0, 0)))
    out_shape = [
        jax.ShapeDtypeStruct((b, length // tq, N_Q_HEADS, KV_WIDTH, tq), BF16),
        jax.ShapeDtypeStruct((b, length, KV_WIDTH), BF16),
        jax.ShapeDtypeStruct((b, N_KV_HEADS, length // tk, V_ROWS, tk), BF16),
        jax.ShapeDtypeStruct((b, length, POOL_WIDTH), F32),
        jax.ShapeDtypeStruct((b, 2, 2, length // 2, FOUR_WIDTH), BF16),
    ]
    out_specs = [
        pl.BlockSpec((1, tm // tq, N_Q_HEADS, KV_WIDTH, tq), lambda i, t: (i, t, 0, 0, 0)),
        pl.BlockSpec((1, tm, KV_WIDTH), lambda i, t: (i, t, 0)),
        pl.BlockSpec((1, N_KV_HEADS, tm // tk, V_ROWS, tk), lambda i, t: (i, 0, t, 0, 0)),
        pl.BlockSpec((1, tm, POOL_WIDTH), lambda i, t: (i, t, 0)),
        pl.BlockSpec((1, 2, 2, tm // 2, FOUR_WIDTH), lambda i, t: (i, 0, 0, t, 0)),
    ]
    in_specs = [
        pl.BlockSpec((1, tm, d), lambda i, t: (i, t, 0)),
        mod, mod,
        _resident((1, d)),
        _resident(wqv.shape), _resident(wr.shape), _resident(hm.shape), _resident(kg.shape),
        pl.BlockSpec((2 * HEAD_DIM, tm), lambda i, t: (0, t)),
        pl.BlockSpec((tm, KV_WIDTH), lambda i, t: (t, 0)),
        pl.BlockSpec((tm, KV_WIDTH), lambda i, t: (t, 0)),
        _resident(ab.shape),
    ]
    return pl.pallas_call(
        functools.partial(_in_proj_kernel, tq=tq, tk=tk),
        out_shape=out_shape, grid=(b, nt), in_specs=in_specs, out_specs=out_specs,
        compiler_params=_params("parallel", "parallel"),
        scratch_shapes=[pltpu.VMEM((2 * FOUR_WIDTH // LANES, tm, LANES), F32)],
        name="in_proj",
    )(x, sh, sc, g, wqv, wr, hm, kg, qtab, kcos, ksin, ab)


def _attn_kernel(*refs, tk, n_ctx, n_lat, ahead, unroll):
    if n_lat:
        q_ref, kc_ref, vc_ref, k_ref, v_ref, o_ref, m_sc, acc_sc = refs
    else:
        q_ref, kc_ref, vc_ref, o_ref, m_sc, acc_sc = refs
    m_sc[...] = jnp.full(m_sc.shape, -jnp.inf, F32)
    acc_sc[...] = jnp.zeros(acc_sc.shape, F32)

    def step(blocks):
        items = [(c, hd) for c in range(len(blocks)) for hd in range(N_Q_HEADS)]
        scores = {}

        def issue(i):
            c, hd = items[i]
            scores[i] = jnp.dot(blocks[c][0], q_ref[0, 0, hd], preferred_element_type=F32)

        for i in range(min(ahead, len(items))):
            issue(i)
        for i, (c, hd) in enumerate(items):
            if i + ahead < len(items):
                issue(i + ahead)
            s = scores.pop(i)
            m_old = m_sc[hd]
            m_new = jnp.maximum(m_old, jnp.max(s, axis=0, keepdims=True))
            alpha = jnp.exp2(m_old - m_new)
            p = jnp.exp2(s - m_new)
            acc_sc[hd] = alpha * acc_sc[hd] + jnp.dot(blocks[c][1][hd // GQA_GROUP], p.astype(BF16),
                                                      preferred_element_type=F32)
            m_sc[hd] = m_new

    def lat_blocks(first):
        blocks = []
        for u in range(unroll):
            idx = first + u
            row = idx * tk if isinstance(idx, int) else pl.multiple_of(idx * tk, tk)
            blocks.append((k_ref[0, pl.ds(row, tk), :], [v_ref[0, g, idx] for g in range(N_KV_HEADS)]))
        return blocks

    ctx_blocks = [(kc_ref[0, c * tk:(c + 1) * tk, :], [vc_ref[0, g, c] for g in range(N_KV_HEADS)])
                  for c in range(n_ctx)]
    if n_lat and unroll == n_lat:
        step(ctx_blocks + lat_blocks(0))
    else:
        step(ctx_blocks)
        if n_lat:
            def body(j, carry):
                step(lat_blocks(j * unroll))
                return carry
            lax.fori_loop(0, n_lat // unroll, body, 0)

    for hd in range(N_Q_HEADS):
        o = acc_sc[hd, 0:HEAD_DIM, :] * (1.0 / acc_sc[hd, HEAD_DIM:HEAD_DIM + 1, :])
        o_ref[0, hd * HEAD_DIM:(hd + 1) * HEAD_DIM, :] = o.astype(BF16)


def _attention(qt, kc, vc, k=None, v=None, *, tk):
    b, nq, _, _, tq = qt.shape
    length = nq * tq
    n_ctx = kc.shape[1] // tk
    n_lat = 0 if k is None else k.shape[1] // tk
    in_specs = [
        pl.BlockSpec((1, 1) + qt.shape[2:], lambda i, j: (i, j, 0, 0, 0)),
        pl.BlockSpec((1,) + kc.shape[1:], lambda i, j: (i, 0, 0)),
        pl.BlockSpec((1,) + vc.shape[1:], lambda i, j: (i, 0, 0, 0, 0)),
    ]
    args = [qt, kc, vc]
    if n_lat:
        in_specs += [pl.BlockSpec((1,) + k.shape[1:], lambda i, j: (i, 0, 0)),
                     pl.BlockSpec((1,) + v.shape[1:], lambda i, j: (i, 0, 0, 0, 0))]
        args += [k, v]
    return pl.pallas_call(
        functools.partial(_attn_kernel, tk=tk, n_ctx=n_ctx, n_lat=n_lat, ahead=ATTN_LOOKAHEAD,
                          unroll=math.gcd(ATTN_UNROLL, max(n_lat, 1))),
        out_shape=jax.ShapeDtypeStruct((b, Q_WIDTH, length), BF16),
        grid=(b, nq),
        in_specs=in_specs,
        out_specs=pl.BlockSpec((1, Q_WIDTH, tq), lambda i, j: (i, 0, j)),
        scratch_shapes=[pltpu.VMEM((N_Q_HEADS, 1, tq), F32), pltpu.VMEM((N_Q_HEADS, V_ROWS, tq), F32)],
        compiler_params=_params("parallel", "parallel"),
        name="attention",
    )(*args)


def _pool_kernel(u_ref, w_ref, s_ref, o_ref, pad_ref, *, chunk):
    length = u_ref.shape[1]
    width = u_ref.shape[2]
    hw = POOL_HALO
    pad_ref[0:hw, :] = jnp.zeros((hw, width), F32)
    pad_ref[hw:hw + length, :] = u_ref[0]
    pad_ref[hw + length:, :] = jnp.zeros((hw, width), F32)
    lane = lax.broadcasted_iota(jnp.int32, (chunk, width), 1)
    win = jnp.full((chunk, width), POOL_WINDOWS[-1], jnp.int32)
    for gi in range(len(POOL_WINDOWS) - 2, -1, -1):
        win = jnp.where(lane < (gi + 1) * POOL_GROUP, POOL_WINDOWS[gi], win)
    half = win // 2
    for c in range(length // chunk):
        r0 = c * chunk
        sums = {}
        acc = None
        lo, hi = hw, hw
        for w in POOL_WINDOWS:
            for off in list(range(hw - w // 2, lo)) + list(range(hi, hw + w // 2)):
                piece = pad_ref[r0 + off:r0 + off + chunk, :]
                acc = piece if acc is None else acc + piece
            lo, hi = hw - w // 2, hw + w // 2
            sums[w] = acc
        s = sums[POOL_WINDOWS[-1]]
        for gi in range(len(POOL_WINDOWS) - 2, -1, -1):
            s = jnp.where(lane < (gi + 1) * POOL_GROUP, sums[POOL_WINDOWS[gi]], s)
        t = r0 + lax.broadcasted_iota(jnp.int32, (chunk, width), 0)
        cnt = jnp.minimum(t + win - half, length) - jnp.maximum(t - half, 0)
        d = s / cnt.astype(F32) - pad_ref[r0 + hw:r0 + hw + chunk, :]
        y = jnp.dot(d.astype(BF16), w_ref[...], preferred_element_type=F32) * s_ref[...]
        o_ref[0, r0:r0 + chunk, :] = y.astype(BF16)


def _pool(u, w_bd, scale, *, chunk):
    b, length, width = u.shape
    return pl.pallas_call(
        functools.partial(_pool_kernel, chunk=chunk),
        out_shape=jax.ShapeDtypeStruct((b, length, width), BF16),
        grid=(b,),
        in_specs=[pl.BlockSpec((1, length, width), lambda i: (i, 0, 0)),
                  _resident(w_bd.shape), _resident(scale.shape)],
        out_specs=pl.BlockSpec((1, length, width), lambda i: (i, 0, 0)),
        scratch_shapes=[pltpu.VMEM((length + 2 * POOL_HALO, width), F32)],
        compiler_params=_params("parallel"),
        name="pool_mix",
    )(u, w_bd, scale)


def _dft_kernel(te_ref, to_ref, v_ref, o_ref):
    e = jnp.dot(te_ref[...], v_ref[0, 0], preferred_element_type=F32)
    o = jnp.dot(to_ref[...], v_ref[0, 1], preferred_element_type=F32)
    o_ref[0, 0] = (e + o).astype(BF16)
    o_ref[0, 1] = (e - o).astype(BF16)


def _dft(tables, vf, *, tl):
    b, _, _, half, width = vf.shape
    v2 = vf.reshape(b, 2, 2 * half, width)
    tab = pl.BlockSpec((tl, 2 * half), lambda i, j: (i, 0))
    y = pl.pallas_call(
        _dft_kernel,
        out_shape=jax.ShapeDtypeStruct((b, 2, half, width), BF16),
        grid=(half // tl, b),
        in_specs=[tab, tab, pl.BlockSpec((1, 2, 2 * half, width), lambda i, j: (j, 0, 0, 0))],
        out_specs=pl.BlockSpec((1, 2, tl, width), lambda i, j: (j, 0, i, 0)),
        compiler_params=_params("parallel", "parallel"),
        name="seq_dft",
    )(tables[0], tables[1], v2)
    return y.reshape(b, 2 * half, width)


def _dft_tables(length):
    half = length // 2
    nc = 64
    na = half // nc
    w = 2.0 * math.pi / length
    k = jnp.arange(half, dtype=jnp.int32)[:, None]
    a = jnp.arange(na, dtype=jnp.int32)[None, :]
    c = jnp.arange(nc, dtype=jnp.int32)[None, :]
    alpha = ((k * (2 * nc * a)) % length).astype(F32) * w
    ca, sa = jnp.cos(alpha)[:, :, None], jnp.sin(alpha)[:, :, None]
    tables = []
    for par in range(2):
        beta = ((k * (2 * c + par)) % length).astype(F32) * w
        cb, sb = jnp.cos(beta)[:, None, :], jnp.sin(beta)[:, None, :]
        cs = (ca * cb - sa * sb).reshape(half, half)
        sn = (sa * cb + ca * sb).reshape(half, half)
        tables.append((jnp.concatenate([cs, sn], axis=1) * (length ** -0.5)).astype(BF16))
    return tables


def _out_ffn_kernel(x_ref, ot_ref, yp_ref, yf_ref, ga1_ref, sh2_ref, sc2_ref, ga2_ref, g_ref,
                    wo_ref, wgu_ref, wd_ref, o_ref, *, fc):
    d_ff = wd_ref.shape[0]
    mix = lax.dot_general(ot_ref[0], wo_ref[0:Q_WIDTH, :], (((0,), (0,)), ((), ())),
                          preferred_element_type=F32)
    mix += jnp.dot(yp_ref[0], wo_ref[Q_WIDTH:Q_WIDTH + POOL_WIDTH, :], preferred_element_type=F32)
    mix += jnp.dot(yf_ref[0], wo_ref[Q_WIDTH + POOL_WIDTH:, :], preferred_element_type=F32)
    x1 = x_ref[0] + ga1_ref[0] * mix
    f = _rms(x1) * g_ref[...]
    fb = (f * (1.0 + sc2_ref[0]) + sh2_ref[0]).astype(BF16)
    acc = jnp.zeros(x1.shape, F32)
    for c in range(d_ff // fc):
        gate = jnp.dot(fb, wgu_ref[:, c * fc:(c + 1) * fc], preferred_element_type=F32)
        up = jnp.dot(fb, wgu_ref[:, d_ff + c * fc:d_ff + (c + 1) * fc], preferred_element_type=F32)
        hc = (gate / (1.0 + jnp.exp(-gate)) * up).astype(BF16)
        acc += jnp.dot(hc, wd_ref[c * fc:(c + 1) * fc, :], preferred_element_type=F32)
    o_ref[0] = x1 + ga2_ref[0] * acc


def _out_ffn(x, ot, yp, yf, ga1, sh2, sc2, ga2, g, wo, wgu, wd, *, tm, fc):
    b, length, d = x.shape
    bm = ga1.shape[0]
    mod = pl.BlockSpec((1, 1, d), (lambda i, t: (i, 0, 0)) if bm > 1 else (lambda i, t: (0, 0, 0)))
    tok = lambda w: pl.BlockSpec((1, tm, w), lambda i, t: (i, t, 0))
    return pl.pallas_call(
        functools.partial(_out_ffn_kernel, fc=fc),
        out_shape=jax.ShapeDtypeStruct(x.shape, F32),
        grid=(b, length // tm),
        in_specs=[tok(d), pl.BlockSpec((1, Q_WIDTH, tm), lambda i, t: (i, 0, t)),
                  tok(POOL_WIDTH), tok(FOUR_WIDTH), mod, mod, mod, mod, _resident((1, d)),
                  _resident(wo.shape), _resident(wgu.shape), _resident(wd.shape)],
        out_specs=tok(d),
        compiler_params=_params("parallel", "parallel"),
        name="out_ffn",
    )(x, ot, yp, yf, ga1, sh2, sc2, ga2, g, wo, wgu, wd)


def _rope_cos_sin(length, rotate):
    if not rotate:
        return jnp.ones((length, 2, N_FREQ), F32), jnp.zeros((length, 2, N_FREQ), F32)
    rows = length // GRID_W
    row_ids = jnp.repeat(jnp.arange(rows), GRID_W).astype(F32)
    col_ids = jnp.tile(jnp.arange(GRID_W), rows).astype(F32)
    freqs = ROPE_THETA ** (-jnp.arange(N_FREQ, dtype=F32) / N_FREQ)
    ang = jnp.stack([row_ids[:, None] * freqs, col_ids[:, None] * freqs], axis=1)
    return jnp.cos(ang), jnp.sin(ang)


def _q_table(cos, sin, q_gain):
    gq = (q_gain * (HEAD_DIM ** -0.5 * math.log2(math.e))).reshape(2, 2, N_FREQ)
    c = jnp.transpose(cos, (1, 2, 0))
    s = jnp.transpose(sin, (1, 2, 0))
    g1, g2 = gq[:, 0, :, None], gq[:, 1, :, None]
    tab = jnp.stack([g1 * c, g2 * s, g2 * c, g1 * s], axis=1)
    return tab.reshape(2 * HEAD_DIM, -1)


def _k_tables(cos, sin):
    c = jnp.concatenate([cos, cos], axis=2).reshape(cos.shape[0], HEAD_DIM)
    s = jnp.concatenate([-sin, sin], axis=2).reshape(sin.shape[0], HEAD_DIM)
    return jnp.tile(c, (1, N_KV_HEADS)), jnp.tile(s, (1, N_KV_HEADS))


def kernel(x, c, ctx, c_ctx, w_ada, b_ada, g_mix, g_ffn, w_in, q_gain, k_gain, w_pool, pool_scale,
           w_four, w_out, w_gate_up, w_down):
    b, s, d = x.shape
    cl = ctx.shape[1]
    depth = w_ada.shape[0]
    d_ff = w_down.shape[1]

    tm_x, tm_c = 512, cl
    tq_x, tq_c = 256, cl
    tk = 128
    tl_x, tl_c = 512, cl // 2
    pool_chunk = 256
    fc = 256

    n_rows = -(-(b + 1) // 8) * 8
    cvec = jnp.zeros((n_rows, d), F32).at[:b].set(c).at[b].set(c_ctx)
    mod = _ada(cvec, w_ada, b_ada)
    gr, gi = _four_weights(w_four)

    cos_x, sin_x = _rope_cos_sin(s, True)
    cos_c, sin_c = _rope_cos_sin(cl, False)
    kt_x = _k_tables(cos_x, sin_x)
    kt_c = _k_tables(cos_c, sin_c)
    dtab_x = _dft_tables(s)
    dtab_c = _dft_tables(cl)
    hm = _block_diag(jnp.full((N_KV_HEADS, HEAD_DIM, HEAD_DIM), 1.0 / HEAD_DIM, F32)).astype(BF16)

    q_end, k_end = Q_WIDTH, Q_WIDTH + KV_WIDTH
    kv_end = k_end + KV_WIDTH
    for i in range(depth):
        update_ctx = i < depth - 1
        wi = w_in[i]
        wqv = jnp.concatenate([wi[:, :q_end], wi[:, k_end:kv_end]], axis=1).T.astype(BF16)
        wr = jnp.concatenate([wi[:, q_end:k_end], wi[:, kv_end:]], axis=1).astype(BF16)
        ab = jnp.concatenate([_block_diag(gr[i]), _block_diag(gi[i])], axis=1).astype(BF16)
        wp = _block_diag(w_pool[i]).astype(BF16)
        ps = pool_scale[i].reshape(1, -1)
        kg = jnp.tile(k_gain[i], N_KV_HEADS).reshape(1, -1)
        wo = w_out[i].astype(BF16)
        wgu = w_gate_up[i].astype(BF16)
        wd = w_down[i].astype(BF16)
        gm = g_mix[i].reshape(1, d)
        gf = g_ffn[i].reshape(1, d)
        mx = [mod[i, :b, j * d:(j + 1) * d].reshape(b, 1, d) for j in range(6)]
        mc = [mod[i, b, j * d:(j + 1) * d].reshape(1, 1, d) for j in range(6)]

        def in_proj(xx, m, cos, sin, kt, tm, tq):
            return _in_proj(xx, m[0], m[1], gm, wqv, wr, hm, kg, _q_table(cos, sin, q_gain[i]),
                            kt[0], kt[1], ab, tm=tm, tq=tq, tk=tk)

        def finish(xx, ot, up, vf, m, dtab, tm, tl):
            yp = _pool(up, wp, ps, chunk=pool_chunk)
            yf = _dft(dtab, vf, tl=tl)
            return _out_ffn(xx, ot, yp, yf, m[2], m[3], m[4], m[5], gf, wo, wgu, wd, tm=tm, fc=fc)

        qt_c, k_c, vt_c, up_c, vf_c = in_proj(ctx, mc, cos_c, sin_c, kt_c, tm_c, tq_c)
        qt_x, k_x, vt_x, up_x, vf_x = in_proj(x, mx, cos_x, sin_x, kt_x, tm_x, tq_x)
        ot_x = _attention(qt_x, k_c, vt_c, k_x, vt_x, tk=tk)
        x = finish(x, ot_x, up_x, vf_x, mx, dtab_x, tm_x, tl_x)
        if update_ctx:
            ot_c = _attention(qt_c, k_c, vt_c, tk=tk)
            ctx = finish(ctx, ot_c, up_c, vf_c, mc, dtab_c, tm_c, tl_c)
    return x
```

```python
import functools
import math

import jax
import jax.numpy as jnp
from jax import lax
from jax.experimental import pallas as pl
from jax.experimental.pallas import tpu as pltpu

F32 = jnp.float32
BF16 = jnp.bfloat16

GRID_W = 64
HEAD_DIM = 64
N_Q_HEADS = 8
N_KV_HEADS = 2
GQA_GROUP = N_Q_HEADS // N_KV_HEADS
Q_WIDTH = N_Q_HEADS * HEAD_DIM
KV_WIDTH = N_KV_HEADS * HEAD_DIM
POOL_WINDOWS = (2, 4, 8, 16)
POOL_WIDTH = 256
POOL_GROUP = 64
FOUR_WIDTH = 256
FOUR_GROUP = 64
N_FREQ = HEAD_DIM // 4
ROPE_THETA = 10000.0
EPS = 1e-6
POOL_HALO = max(POOL_WINDOWS) // 2
LANES = 128
V_ROWS = HEAD_DIM + 16
ATTN_LOOKAHEAD = 5
ATTN_UNROLL = 32

VMEM_LIMIT_BYTES = 56 * 1024 * 1024


def _params(*sem):
    return pltpu.CompilerParams(dimension_semantics=sem, vmem_limit_bytes=VMEM_LIMIT_BYTES)


def _resident(shape):
    zeros = (0,) * len(shape)
    return pl.BlockSpec(shape, lambda *_: zeros, pipeline_mode=pl.Buffered(1))


def _rms(x):
    return x * lax.rsqrt(jnp.mean(x * x, axis=-1, keepdims=True) + EPS)


def _ada_kernel(c_ref, w_ref, b_ref, o_ref):
    c = c_ref[...]
    sc = c / (1.0 + jnp.exp(-c))
    o_ref[0] = jnp.dot(sc, w_ref[0], preferred_element_type=F32,
                       precision=lax.Precision.HIGHEST) + b_ref[0]


def _ada(cvec, w_ada, b_ada):
    depth, d, n = w_ada.shape
    r = cvec.shape[0]
    tn = 1536
    return pl.pallas_call(
        _ada_kernel,
        out_shape=jax.ShapeDtypeStruct((depth, r, n), F32),
        grid=(depth, n // tn),
        in_specs=[pl.BlockSpec((r, d), lambda i, j: (0, 0)),
                  pl.BlockSpec((1, d, tn), lambda i, j: (i, 0, j)),
                  pl.BlockSpec((1, 1, tn), lambda i, j: (i, 0, j))],
        out_specs=pl.BlockSpec((1, r, tn), lambda i, j: (i, 0, j)),
        compiler_params=_params("arbitrary", "arbitrary"),
        name="ada_mod",
    )(cvec, w_ada, b_ada.reshape(depth, 1, n))


def _four_w_kernel(c_ref, s_ref, w_ref, gr_ref, gi_ref):
    for g in range(w_ref.shape[1]):
        w = w_ref[0, g]
        gr_ref[0, g] = jnp.dot(c_ref[...], w, preferred_element_type=F32, precision=lax.Precision.HIGHEST)
        gi_ref[0, g] = -jnp.dot(s_ref[...], w, preferred_element_type=F32, precision=lax.Precision.HIGHEST)


def _four_weights(w_four):
    depth, ng, gc, _ = w_four.shape
    c = jnp.arange(gc, dtype=jnp.int32)
    ang = ((c[:, None] * c[None, :]) % gc).astype(F32) * (2.0 * math.pi / gc)
    scale = gc ** -0.5
    spec = pl.BlockSpec((1, ng, gc, gc), lambda i: (i, 0, 0, 0))
    tab = pl.BlockSpec((gc, gc), lambda i: (0, 0))
    gr, gi = pl.pallas_call(
        _four_w_kernel,
        out_shape=[jax.ShapeDtypeStruct(w_four.shape, F32)] * 2,
        grid=(depth,),
        in_specs=[tab, tab, spec],
        out_specs=[spec, spec],
        compiler_params=_params("arbitrary"),
        name="four_weights",
    )(jnp.cos(ang) * scale, jnp.sin(ang) * scale, w_four)
    return gr, gi


def _block_diag(blocks):
    g, a, b = blocks.shape
    eye = jnp.eye(g, dtype=blocks.dtype)
    return (eye[:, None, :, None] * blocks[:, :, None, :]).reshape(g * a, g * b)


def _in_proj_kernel(x_ref, sh_ref, sc_ref, g_ref, wqv_ref, wr_ref, hm_ref, kg_ref, qtab_ref,
                    kcos_ref, ksin_ref, ab_ref, q_ref, k_ref, v_ref, up_ref, vf_ref, vf_sc, *, tq, tk):
    tm = x_ref.shape[1]
    h = _rms(x_ref[0]) * g_ref[...]
    h = h * (1.0 + sc_ref[0]) + sh_ref[0]
    hb = h.astype(BF16)
    zt = lax.dot_general(wqv_ref[...], hb, (((1,), (1,)), ((), ())), preferred_element_type=F32)
    zr = jnp.dot(hb, wr_ref[...], preferred_element_type=F32)

    tab = qtab_ref[...]
    for hd in range(N_Q_HEADS):
        qh = zt[hd * HEAD_DIM:(hd + 1) * HEAD_DIM, :]
        ss = jnp.sum(qh * qh, axis=0, keepdims=True) * (1.0 / HEAD_DIM)
        qn = qh * lax.rsqrt(ss + EPS)
        parts = []
        for seg in range(2):
            x1 = qn[seg * 32:seg * 32 + N_FREQ]
            x2 = qn[seg * 32 + N_FREQ:seg * 32 + 2 * N_FREQ]
            t0 = seg * 64
            g1c, g2s = tab[t0:t0 + 16], tab[t0 + 16:t0 + 32]
            g2c, g1s = tab[t0 + 32:t0 + 48], tab[t0 + 48:t0 + 64]
            parts += [x1 * g1c - x2 * g2s, x2 * g2c + x1 * g1s]
        qr = jnp.concatenate(parts, axis=0).astype(BF16)
        g = hd // GQA_GROUP
        o = 1 - g
        for j in range(tm // tq):
            q_ref[0, j, hd, g * HEAD_DIM:(g + 1) * HEAD_DIM, :] = qr[:, j * tq:(j + 1) * tq]
            q_ref[0, j, hd, o * HEAD_DIM:(o + 1) * HEAD_DIM, :] = jnp.zeros((HEAD_DIM, tq), BF16)

    vt = zt[Q_WIDTH:Q_WIDTH + KV_WIDTH].astype(BF16)
    for g in range(N_KV_HEADS):
        for j in range(tm // tk):
            v_ref[0, g, j, 0:HEAD_DIM, :] = vt[g * HEAD_DIM:(g + 1) * HEAD_DIM, j * tk:(j + 1) * tk]
            v_ref[0, g, j, HEAD_DIM:, :] = jnp.ones((V_ROWS - HEAD_DIM, tk), BF16)

    kz = zr[:, :KV_WIDTH]
    ssk = jnp.dot((kz * kz).astype(BF16), hm_ref[...], preferred_element_type=F32)
    kn = kz * lax.rsqrt(ssk + EPS) * kg_ref[...]
    lane = lax.broadcasted_iota(jnp.int32, kn.shape, 1)
    rot = jnp.where((lane % 32) < N_FREQ,
                    pltpu.roll(kn, KV_WIDTH - N_FREQ, 1), pltpu.roll(kn, N_FREQ, 1))
    k_ref[0] = (kn * kcos_ref[...] + rot * ksin_ref[...]).astype(BF16)

    up_ref[0] = zr[:, KV_WIDTH:KV_WIDTH + POOL_WIDTH]
    uf = zr[:, KV_WIDTH + POOL_WIDTH:].astype(BF16)
    v2 = jnp.dot(uf, ab_ref[...], preferred_element_type=F32)
    n_chunks = 2 * FOUR_WIDTH // LANES
    per_part = FOUR_WIDTH // LANES
    for ch in range(n_chunks):
        vf_sc[ch] = v2[:, ch * LANES:(ch + 1) * LANES]
    for par in range(2):
        for ch in range(n_chunks):
            rows = vf_sc[ch, pl.ds(par, tm // 2, stride=2), :]
            part, col = divmod(ch, per_part)
            vf_ref[0, par, part, :, col * LANES:(col + 1) * LANES] = rows.astype(BF16)


def _in_proj(x, sh, sc, g, wqv, wr, hm, kg, qtab, kcos, ksin, ab, *, tm, tq, tk):
    b, length, d = x.shape
    nt = length // tm
    bm = sh.shape[0]
    mod = pl.BlockSpec((1, 1, d), (lambda i, t: (i, 0, 0)) if bm > 1 else (lambda i, t: (0, 0, 0)))
    out_shape = [
        jax.ShapeDtypeStruct((b, length // tq, N_Q_HEADS, KV_WIDTH, tq), BF16),
        jax.ShapeDtypeStruct((b, length, KV_WIDTH), BF16),
        jax.ShapeDtypeStruct((b, N_KV_HEADS, length // tk, V_ROWS, tk), BF16),
        jax.ShapeDtypeStruct((b, length, POOL_WIDTH), F32),
        jax.ShapeDtypeStruct((b, 2, 2, length // 2, FOUR_WIDTH), BF16),
    ]
    out_specs = [
        pl.BlockSpec((1, tm // tq, N_Q_HEADS, KV_WIDTH, tq), lambda i, t: (i, t, 0, 0, 0)),
        pl.BlockSpec((1, tm, KV_WIDTH), lambda i, t: (i, t, 0)),
        pl.BlockSpec((1, N_KV_HEADS, tm // tk, V_ROWS, tk), lambda i, t: (i, 0, t, 0, 0)),
        pl.BlockSpec((1, tm, POOL_WIDTH), lambda i, t: (i, t, 0)),
        pl.BlockSpec((1, 2, 2, tm // 2, FOUR_WIDTH), lambda i, t: (i, 0, 0, t, 0)),
    ]
    in_specs = [
        pl.BlockSpec((1, tm, d), lambda i, t: (i, t, 0)),
        mod, mod,
        _resident((1, d)),
        _resident(wqv.shape), _resident(wr.shape), _resident(hm.shape), _resident(kg.shape),
        pl.BlockSpec((2 * HEAD_DIM, tm), lambda i, t: (0, t)),
        pl.BlockSpec((tm, KV_WIDTH), lambda i, t: (t, 0)),
        pl.BlockSpec((tm, KV_WIDTH), lambda i, t: (t, 0)),
        _resident(ab.shape),
    ]
    return pl.pallas_call(
        functools.partial(_in_proj_kernel, tq=tq, tk=tk),
        out_shape=out_shape, grid=(b, nt), in_specs=in_specs, out_specs=out_specs,
        compiler_params=_params("parallel", "parallel"),
        scratch_shapes=[pltpu.VMEM((2 * FOUR_WIDTH // LANES, tm, LANES), F32)],
        name="in_proj",
    )(x, sh, sc, g, wqv, wr, hm, kg, qtab, kcos, ksin, ab)


def _attn_kernel(*refs, tk, n_ctx, n_lat, ahead, unroll):
    if n_lat:
        q_ref, kc_ref, vc_ref, k_ref, v_ref, o_ref, m_sc, acc_sc = refs
    else:
        q_ref, kc_ref, vc_ref, o_ref, m_sc, acc_sc = refs
    m_sc[...] = jnp.full(m_sc.shape, -jnp.inf, F32)
    acc_sc[...] = jnp.zeros(acc_sc.shape, F32)

    def step(blocks):
        items = [(c, hd) for c in range(len(blocks)) for hd in range(N_Q_HEADS)]
        scores = {}

        def issue(i):
            c, hd = items[i]
            scores[i] = jnp.dot(blocks[c][0], q_ref[0, 0, hd], preferred_element_type=F32)

        for i in range(min(ahead, len(items))):
            issue(i)
        for i, (c, hd) in enumerate(items):
            if i + ahead < len(items):
                issue(i + ahead)
            s = scores.pop(i)
            m_old = m_sc[hd]
            m_new = jnp.maximum(m_old, jnp.max(s, axis=0, keepdims=True))
            alpha = jnp.exp2(m_old - m_new)
            p = jnp.exp2(s - m_new)
            acc_sc[hd] = alpha * acc_sc[hd] + jnp.dot(blocks[c][1][hd // GQA_GROUP], p.astype(BF16),
                                                      preferred_element_type=F32)
            m_sc[hd] = m_new

    def lat_blocks(first):
        blocks = []
        for u in range(unroll):
            idx = first + u
            row = idx * tk if isinstance(idx, int) else pl.multiple_of(idx * tk, tk)
            blocks.append((k_ref[0, pl.ds(row, tk), :], [v_ref[0, g, idx] for g in range(N_KV_HEADS)]))
        return blocks

    ctx_blocks = [(kc_ref[0, c * tk:(c + 1) * tk, :], [vc_ref[0, g, c] for g in range(N_KV_HEADS)])
                  for c in range(n_ctx)]
    if n_lat and unroll == n_lat:
        step(ctx_blocks + lat_blocks(0))
    else:
        step(ctx_blocks)
        if n_lat:
            def body(j, carry):
                step(lat_blocks(j * unroll))
                return carry
            lax.fori_loop(0, n_lat // unroll, body, 0)

    for hd in range(N_Q_HEADS):
        o = acc_sc[hd, 0:HEAD_DIM, :] * (1.0 / acc_sc[hd, HEAD_DIM:HEAD_DIM + 1, :])
        o_ref[0, hd * HEAD_DIM:(hd + 1) * HEAD_DIM, :] = o.astype(BF16)


def _attention(qt, kc, vc, k=None, v=None, *, tk):
    b, nq, _, _, tq = qt.shape
    length = nq * tq
    n_ctx = kc.shape[1] // tk
    n_lat = 0 if k is None else k.shape[1] // tk
    in_specs = [
        pl.BlockSpec((1, 1) + qt.shape[2:], lambda i, j: (i, j, 0, 0, 0)),
        pl.BlockSpec((1,) + kc.shape[1:], lambda i, j: (i, 0, 0)),
        pl.BlockSpec((1,) + vc.shape[1:], lambda i, j: (i, 0, 0, 0, 0)),
    ]
    args = [qt, kc, vc]
    if n_lat:
        in_specs += [pl.BlockSpec((1,) + k.shape[1:], lambda i, j: (i, 0, 0)),
                     pl.BlockSpec((1,) + v.shape[1:], lambda i, j: (i, 0, 0, 0, 0))]
        args += [k, v]
    return pl.pallas_call(
        functools.partial(_attn_kernel, tk=tk, n_ctx=n_ctx, n_lat=n_lat, ahead=ATTN_LOOKAHEAD,
                          unroll=math.gcd(ATTN_UNROLL, max(n_lat, 1))),
        out_shape=jax.ShapeDtypeStruct((b, Q_WIDTH, length), BF16),
        grid=(b, nq),
        in_specs=in_specs,
        out_specs=pl.BlockSpec((1, Q_WIDTH, tq), lambda i, j: (i, 0, j)),
        scratch_shapes=[pltpu.VMEM((N_Q_HEADS, 1, tq), F32), pltpu.VMEM((N_Q_HEADS, V_ROWS, tq), F32)],
        compiler_params=_params("parallel", "parallel"),
        name="attention",
    )(*args)


def _pool_kernel(u_ref, w_ref, s_ref, o_ref, pad_ref, *, chunk):
    length = u_ref.shape[1]
    width = u_ref.shape[2]
    hw = POOL_HALO
    pad_ref[0:hw, :] = jnp.zeros((hw, width), F32)
    pad_ref[hw:hw + length, :] = u_ref[0]
    pad_ref[hw + length:, :] = jnp.zeros((hw, width), F32)
    assert 2 * POOL_GROUP == LANES
    first = lax.broadcasted_iota(jnp.int32, (chunk, LANES), 1) < POOL_GROUP
    row = lax.broadcasted_iota(jnp.int32, (chunk, LANES), 0)
    for c in range(length // chunk):
        r0 = c * chunk
        clipped = r0 < hw or r0 + chunk + hw > length
        cols = []
        for col in range(width // LANES):
            wins = POOL_WINDOWS[2 * col:2 * col + 2]
            cl = slice(col * LANES, (col + 1) * LANES)
            sums = []
            acc = None
            lo, hi = hw, hw
            for w in wins:
                for off in list(range(hw - w // 2, lo)) + list(range(hi, hw + w // 2)):
                    piece = pad_ref[r0 + off:r0 + off + chunk, cl]
                    acc = piece if acc is None else acc + piece
                lo, hi = hw - w // 2, hw + w // 2
                sums.append(acc)
            s = jnp.where(first, sums[0], sums[1])
            if clipped:
                win = jnp.where(first, wins[0], wins[1])
                t = r0 + row
                cnt = jnp.minimum(t + win - win // 2, length) - jnp.maximum(t - win // 2, 0)
                mean = s / cnt.astype(F32)
            else:
                mean = s * jnp.where(first, 1.0 / wins[0], 1.0 / wins[1])
            cols.append(mean - pad_ref[r0 + hw:r0 + hw + chunk, cl])
        d = jnp.concatenate(cols, axis=1)
        y = jnp.dot(d.astype(BF16), w_ref[...], preferred_element_type=F32) * s_ref[...]
        o_ref[0, r0:r0 + chunk, :] = y.astype(BF16)


def _pool(u, w_bd, scale, *, chunk):
    b, length, width = u.shape
    return pl.pallas_call(
        functools.partial(_pool_kernel, chunk=chunk),
        out_shape=jax.ShapeDtypeStruct((b, length, width), BF16),
        grid=(b,),
        in_specs=[pl.BlockSpec((1, length, width), lambda i: (i, 0, 0)),
                  _resident(w_bd.shape), _resident(scale.shape)],
        out_specs=pl.BlockSpec((1, length, width), lambda i: (i, 0, 0)),
        scratch_shapes=[pltpu.VMEM((length + 2 * POOL_HALO, width), F32)],
        compiler_params=_params("parallel"),
        name="pool_mix",
    )(u, w_bd, scale)


def _dft_kernel(te_ref, to_ref, v_ref, o_ref):
    e = jnp.dot(te_ref[...], v_ref[0, 0], preferred_element_type=F32)
    o = jnp.dot(to_ref[...], v_ref[0, 1], preferred_element_type=F32)
    o_ref[0, 0] = (e + o).astype(BF16)
    o_ref[0, 1] = (e - o).astype(BF16)


def _dft(tables, vf, *, tl):
    b, _, _, half, width = vf.shape
    v2 = vf.reshape(b, 2, 2 * half, width)
    tab = pl.BlockSpec((tl, 2 * half), lambda i, j: (i, 0))
    y = pl.pallas_call(
        _dft_kernel,
        out_shape=jax.ShapeDtypeStruct((b, 2, half, width), BF16),
        grid=(half // tl, b),
        in_specs=[tab, tab, pl.BlockSpec((1, 2, 2 * half, width), lambda i, j: (j, 0, 0, 0))],
        out_specs=pl.BlockSpec((1, 2, tl, width), lambda i, j: (j, 0, i, 0)),
        compiler_params=_params("parallel", "parallel"),
        name="seq_dft",
    )(tables[0], tables[1], v2)
    return y.reshape(b, 2 * half, width)


def _dft_tables(length):
    half = length // 2
    nc = 64
    na = half // nc
    w = 2.0 * math.pi / length
    k = jnp.arange(half, dtype=jnp.int32)[:, None]
    a = jnp.arange(na, dtype=jnp.int32)[None, :]
    c = jnp.arange(nc, dtype=jnp.int32)[None, :]
    alpha = ((k * (2 * nc * a)) % length).astype(F32) * w
    ca, sa = jnp.cos(alpha)[:, :, None], jnp.sin(alpha)[:, :, None]
    tables = []
    for par in range(2):
        beta = ((k * (2 * c + par)) % length).astype(F32) * w
        cb, sb = jnp.cos(beta)[:, None, :], jnp.sin(beta)[:, None, :]
        cs = (ca * cb - sa * sb).reshape(half, half)
        sn = (sa * cb + ca * sb).reshape(half, half)
        tables.append((jnp.concatenate([cs, sn], axis=1) * (length ** -0.5)).astype(BF16))
    return tables


def _out_ffn_kernel(x_ref, ot_ref, yp_ref, yf_ref, ga1_ref, sh2_ref, sc2_ref, ga2_ref, g_ref,
                    wo_ref, wgu_ref, wd_ref, o_ref, *, fc):
    d_ff = wd_ref.shape[0]
    mix = lax.dot_general(ot_ref[0], wo_ref[0:Q_WIDTH, :], (((0,), (0,)), ((), ())),
                          preferred_element_type=F32)
    mix += jnp.dot(yp_ref[0], wo_ref[Q_WIDTH:Q_WIDTH + POOL_WIDTH, :], preferred_element_type=F32)
    mix += jnp.dot(yf_ref[0], wo_ref[Q_WIDTH + POOL_WIDTH:, :], preferred_element_type=F32)
    x1 = x_ref[0] + ga1_ref[0] * mix
    f = _rms(x1) * g_ref[...]
    fb = (f * (1.0 + sc2_ref[0]) + sh2_ref[0]).astype(BF16)
    acc = jnp.zeros(x1.shape, F32)
    for c in range(d_ff // fc):
        gate = jnp.dot(fb, wgu_ref[:, c * fc:(c + 1) * fc], preferred_element_type=F32)
        up = jnp.dot(fb, wgu_ref[:, d_ff + c * fc:d_ff + (c + 1) * fc], preferred_element_type=F32)
        hc = (gate / (1.0 + jnp.exp(-gate)) * up).astype(BF16)
        acc += jnp.dot(hc, wd_ref[c * fc:(c + 1) * fc, :], preferred_element_type=F32)
    o_ref[0] = x1 + ga2_ref[0] * acc


def _out_ffn(x, ot, yp, yf, ga1, sh2, sc2, ga2, g, wo, wgu, wd, *, tm, fc):
    b, length, d = x.shape
    bm = ga1.shape[0]
    mod = pl.BlockSpec((1, 1, d), (lambda i, t: (i, 0, 0)) if bm > 1 else (lambda i, t: (0, 0, 0)))
    tok = lambda w: pl.BlockSpec((1, tm, w), lambda i, t: (i, t, 0))
    return pl.pallas_call(
        functools.partial(_out_ffn_kernel, fc=fc),
        out_shape=jax.ShapeDtypeStruct(x.shape, F32),
        grid=(b, length // tm),
        in_specs=[tok(d), pl.BlockSpec((1, Q_WIDTH, tm), lambda i, t: (i, 0, t)),
                  tok(POOL_WIDTH), tok(FOUR_WIDTH), mod, mod, mod, mod, _resident((1, d)),
                  _resident(wo.shape), _resident(wgu.shape), _resident(wd.shape)],
        out_specs=tok(d),
        compiler_params=_params("parallel", "parallel"),
        name="out_ffn",
    )(x, ot, yp, yf, ga1, sh2, sc2, ga2, g, wo, wgu, wd)


def _rope_cos_sin(length, rotate):
    if not rotate:
        return jnp.ones((length, 2, N_FREQ), F32), jnp.zeros((length, 2, N_FREQ), F32)
    rows = length // GRID_W
    row_ids = jnp.repeat(jnp.arange(rows), GRID_W).astype(F32)
    col_ids = jnp.tile(jnp.arange(GRID_W), rows).astype(F32)
    freqs = ROPE_THETA ** (-jnp.arange(N_FREQ, dtype=F32) / N_FREQ)
    ang = jnp.stack([row_ids[:, None] * freqs, col_ids[:, None] * freqs], axis=1)
    return jnp.cos(ang), jnp.sin(ang)


def _q_table(cos, sin, q_gain):
    gq = (q_gain * (HEAD_DIM ** -0.5 * math.log2(math.e))).reshape(2, 2, N_FREQ)
    c = jnp.transpose(cos, (1, 2, 0))
    s = jnp.transpose(sin, (1, 2, 0))
    g1, g2 = gq[:, 0, :, None], gq[:, 1, :, None]
    tab = jnp.stack([g1 * c, g2 * s, g2 * c, g1 * s], axis=1)
    return tab.reshape(2 * HEAD_DIM, -1)


def _k_tables(cos, sin):
    c = jnp.concatenate([cos, cos], axis=2).reshape(cos.shape[0], HEAD_DIM)
    s = jnp.concatenate([-sin, sin], axis=2).reshape(sin.shape[0], HEAD_DIM)
    return jnp.tile(c, (1, N_KV_HEADS)), jnp.tile(s, (1, N_KV_HEADS))


def kernel(x, c, ctx, c_ctx, w_ada, b_ada, g_mix, g_ffn, w_in, q_gain, k_gain, w_pool, pool_scale,
           w_four, w_out, w_gate_up, w_down):
    b, s, d = x.shape
    cl = ctx.shape[1]
    depth = w_ada.shape[0]
    d_ff = w_down.shape[1]

    tm_x, tm_c = 512, cl
    tmi_x, tmi_c = 1024, cl
    tq_x, tq_c = 256, cl
    tk = 256
    tl_x, tl_c = 512, cl // 2
    pool_chunk = 256
    fc = 256

    n_rows = -(-(b + 1) // 8) * 8
    cvec = jnp.zeros((n_rows, d), F32).at[:b].set(c).at[b].set(c_ctx)
    mod = _ada(cvec, w_ada, b_ada)
    gr, gi = _four_weights(w_four)

    cos_x, sin_x = _rope_cos_sin(s, True)
    cos_c, sin_c = _rope_cos_sin(cl, False)
    kt_x = _k_tables(cos_x, sin_x)
    kt_c = _k_tables(cos_c, sin_c)
    dtab_x = _dft_tables(s)
    dtab_c = _dft_tables(cl)
    hm = _block_diag(jnp.full((N_KV_HEADS, HEAD_DIM, HEAD_DIM), 1.0 / HEAD_DIM, F32)).astype(BF16)

    q_end, k_end = Q_WIDTH, Q_WIDTH + KV_WIDTH
    kv_end = k_end + KV_WIDTH
    for i in range(depth):
        update_ctx = i < depth - 1
        wi = w_in[i]
        wqv = jnp.concatenate([wi[:, :q_end], wi[:, k_end:kv_end]], axis=1).T.astype(BF16)
        wr = jnp.concatenate([wi[:, q_end:k_end], wi[:, kv_end:]], axis=1).astype(BF16)
        ab = jnp.concatenate([_block_diag(gr[i]), _block_diag(gi[i])], axis=1).astype(BF16)
        wp = _block_diag(w_pool[i]).astype(BF16)
        ps = pool_scale[i].reshape(1, -1)
        kg = jnp.tile(k_gain[i], N_KV_HEADS).reshape(1, -1)
        wo = w_out[i].astype(BF16)
        wgu = w_gate_up[i].astype(BF16)
        wd = w_down[i].astype(BF16)
        gm = g_mix[i].reshape(1, d)
        gf = g_ffn[i].reshape(1, d)
        mx = [mod[i, :b, j * d:(j + 1) * d].reshape(b, 1, d) for j in range(6)]
        mc = [mod[i, b, j * d:(j + 1) * d].reshape(1, 1, d) for j in range(6)]

        def in_proj(xx, m, cos, sin, kt, tm, tq):
            return _in_proj(xx, m[0], m[1], gm, wqv, wr, hm, kg, _q_table(cos, sin, q_gain[i]),
                            kt[0], kt[1], ab, tm=tm, tq=tq, tk=tk)

        def finish(xx, ot, up, vf, m, dtab, tm, tl):
            yp = _pool(up, wp, ps, chunk=pool_chunk)
            yf = _dft(dtab, vf, tl=tl)
            return _out_ffn(xx, ot, yp, yf, m[2], m[3], m[4], m[5], gf, wo, wgu, wd, tm=tm, fc=fc)

        qt_c, k_c, vt_c, up_c, vf_c = in_proj(ctx, mc, cos_c, sin_c, kt_c, tmi_c, tq_c)
        qt_x, k_x, vt_x, up_x, vf_x = in_proj(x, mx, cos_x, sin_x, kt_x, tmi_x, tq_x)
        ot_x = _attention(qt_x, k_c, vt_c, k_x, vt_x, tk=tk)
        x = finish(x, ot_x, up_x, vf_x, mx, dtab_x, tm_x, tl_x)
        if update_ctx:
            ot_c = _attention(qt_c, k_c, vt_c, tk=tk)
            ctx = finish(ctx, ot_c, up_c, vf_c, mc, dtab_c, tm_c, tl_c)
    return x
```

```python
import functools
import math

import jax
import jax.numpy as jnp
from jax import lax
from jax.experimental import pallas as pl
from jax.experimental.pallas import tpu as pltpu

F32 = jnp.float32
BF16 = jnp.bfloat16

GRID_W = 64
HEAD_DIM = 64
N_Q_HEADS = 8
N_KV_HEADS = 2
GQA_GROUP = N_Q_HEADS // N_KV_HEADS
Q_WIDTH = N_Q_HEADS * HEAD_DIM
KV_WIDTH = N_KV_HEADS * HEAD_DIM
POOL_WINDOWS = (2, 4, 8, 16)
POOL_WIDTH = 256
POOL_GROUP = 64
FOUR_WIDTH = 256
FOUR_GROUP = 64
N_FREQ = HEAD_DIM // 4
ROPE_THETA = 10000.0
EPS = 1e-6
POOL_HALO = max(POOL_WINDOWS) // 2
LANES = 128
V_ROWS = HEAD_DIM + 16
ATTN_LOOKAHEAD = 5
ATTN_TILES = 1

VMEM_LIMIT_BYTES = 56 * 1024 * 1024


def _params(*sem):
    return pltpu.CompilerParams(dimension_semantics=sem, vmem_limit_bytes=VMEM_LIMIT_BYTES)


def _resident(shape):
    zeros = (0,) * len(shape)
    return pl.BlockSpec(shape, lambda *_: zeros, pipeline_mode=pl.Buffered(1))


def _rms(x):
    return x * lax.rsqrt(jnp.mean(x * x, axis=-1, keepdims=True) + EPS)


def _ada_kernel(c_ref, w_ref, b_ref, o_ref):
    c = c_ref[...]
    sc = c / (1.0 + jnp.exp(-c))
    o_ref[0] = jnp.dot(sc, w_ref[0], preferred_element_type=F32,
                       precision=lax.Precision.HIGHEST) + b_ref[0]


def _ada(cvec, w_ada, b_ada):
    depth, d, n = w_ada.shape
    r = cvec.shape[0]
    tn = 1536
    return pl.pallas_call(
        _ada_kernel,
        out_shape=jax.ShapeDtypeStruct((depth, r, n), F32),
        grid=(depth, n // tn),
        in_specs=[pl.BlockSpec((r, d), lambda i, j: (0, 0)),
                  pl.BlockSpec((1, d, tn), lambda i, j: (i, 0, j)),
                  pl.BlockSpec((1, 1, tn), lambda i, j: (i, 0, j))],
        out_specs=pl.BlockSpec((1, r, tn), lambda i, j: (i, 0, j)),
        compiler_params=_params("arbitrary", "arbitrary"),
        name="ada_mod",
    )(cvec, w_ada, b_ada.reshape(depth, 1, n))


def _four_w_kernel(c_ref, s_ref, w_ref, gr_ref, gi_ref):
    for g in range(w_ref.shape[1]):
        w = w_ref[0, g]
        gr_ref[0, g] = jnp.dot(c_ref[...], w, preferred_element_type=F32, precision=lax.Precision.HIGHEST)
        gi_ref[0, g] = -jnp.dot(s_ref[...], w, preferred_element_type=F32, precision=lax.Precision.HIGHEST)


def _four_weights(w_four):
    depth, ng, gc, _ = w_four.shape
    c = jnp.arange(gc, dtype=jnp.int32)
    ang = ((c[:, None] * c[None, :]) % gc).astype(F32) * (2.0 * math.pi / gc)
    scale = gc ** -0.5
    spec = pl.BlockSpec((1, ng, gc, gc), lambda i: (i, 0, 0, 0))
    tab = pl.BlockSpec((gc, gc), lambda i: (0, 0))
    gr, gi = pl.pallas_call(
        _four_w_kernel,
        out_shape=[jax.ShapeDtypeStruct(w_four.shape, F32)] * 2,
        grid=(depth,),
        in_specs=[tab, tab, spec],
        out_specs=[spec, spec],
        compiler_params=_params("arbitrary"),
        name="four_weights",
    )(jnp.cos(ang) * scale, jnp.sin(ang) * scale, w_four)
    return gr, gi


def _block_diag(blocks):
    g, a, b = blocks.shape
    eye = jnp.eye(g, dtype=blocks.dtype)
    return (eye[:, None, :, None] * blocks[:, :, None, :]).reshape(g * a, g * b)


def _in_proj_kernel(x_ref, sh_ref, sc_ref, g_ref, wqv_ref, wr_ref, hm_ref, kg_ref, qtab_ref,
                    kcos_ref, ksin_ref, ab_ref, q_ref, k_ref, v_ref, up_ref, vf_ref, vf_sc, *, ts):
    tm = x_ref.shape[1]

    def project(j):
        rows = slice(j * ts, (j + 1) * ts)
        h = _rms(x_ref[0, rows, :]) * g_ref[...]
        h = h * (1.0 + sc_ref[0]) + sh_ref[0]
        hb = h.astype(BF16)
        zt = lax.dot_general(wqv_ref[...], hb, (((1,), (1,)), ((), ())), preferred_element_type=F32)
        zr = jnp.dot(hb, wr_ref[...], preferred_element_type=F32)
        return zt, zr

    def emit(j, zt, zr):
        rows = slice(j * ts, (j + 1) * ts)
        tab = qtab_ref[:, rows]
        for hd in range(N_Q_HEADS):
            qh = zt[hd * HEAD_DIM:(hd + 1) * HEAD_DIM, :]
            ss = jnp.sum(qh * qh, axis=0, keepdims=True) * (1.0 / HEAD_DIM)
            qn = qh * lax.rsqrt(ss + EPS)
            parts = []
            for seg in range(2):
                x1 = qn[seg * 32:seg * 32 + N_FREQ]
                x2 = qn[seg * 32 + N_FREQ:seg * 32 + 2 * N_FREQ]
                t0 = seg * 64
                g1c, g2s = tab[t0:t0 + 16], tab[t0 + 16:t0 + 32]
                g2c, g1s = tab[t0 + 32:t0 + 48], tab[t0 + 48:t0 + 64]
                parts += [x1 * g1c - x2 * g2s, x2 * g2c + x1 * g1s]
            g = hd // GQA_GROUP
            o = 1 - g
            q_ref[0, j, hd, g * HEAD_DIM:(g + 1) * HEAD_DIM, :] = jnp.concatenate(parts, axis=0).astype(BF16)
            q_ref[0, j, hd, o * HEAD_DIM:(o + 1) * HEAD_DIM, :] = jnp.zeros((HEAD_DIM, ts), BF16)

        vt = zt[Q_WIDTH:Q_WIDTH + KV_WIDTH].astype(BF16)
        for g in range(N_KV_HEADS):
            v_ref[0, g, j, 0:HEAD_DIM, :] = vt[g * HEAD_DIM:(g + 1) * HEAD_DIM, :]
            v_ref[0, g, j, HEAD_DIM:, :] = jnp.ones((V_ROWS - HEAD_DIM, ts), BF16)

        kz = zr[:, :KV_WIDTH]
        ssk = jnp.dot((kz * kz).astype(BF16), hm_ref[...], preferred_element_type=F32)
        kn = kz * lax.rsqrt(ssk + EPS) * kg_ref[...]
        lane = lax.broadcasted_iota(jnp.int32, kn.shape, 1)
        rot = jnp.where((lane % 32) < N_FREQ,
                        pltpu.roll(kn, KV_WIDTH - N_FREQ, 1), pltpu.roll(kn, N_FREQ, 1))
        k_ref[0, rows, :] = (kn * kcos_ref[rows, :] + rot * ksin_ref[rows, :]).astype(BF16)

        up_ref[0, rows, :] = zr[:, KV_WIDTH:KV_WIDTH + POOL_WIDTH]
        uf = zr[:, KV_WIDTH + POOL_WIDTH:].astype(BF16)
        v2 = jnp.dot(uf, ab_ref[...], preferred_element_type=F32)
        n_chunks = 2 * FOUR_WIDTH // LANES
        per_part = FOUR_WIDTH // LANES
        half = slice(j * ts // 2, (j + 1) * ts // 2)
        for ch in range(n_chunks):
            vf_sc[ch] = v2[:, ch * LANES:(ch + 1) * LANES]
        for par in range(2):
            for ch in range(n_chunks):
                picked = vf_sc[ch, pl.ds(par, ts // 2, stride=2), :]
                part, col = divmod(ch, per_part)
                vf_ref[0, par, part, half, col * LANES:(col + 1) * LANES] = picked.astype(BF16)

    pending = project(0)
    for j in range(tm // ts):
        following = project(j + 1) if (j + 1) * ts < tm else None
        emit(j, *pending)
        pending = following


def _in_proj(x, sh, sc, g, wqv, wr, hm, kg, qtab, kcos, ksin, ab, *, tm, ts):
    tq = tk = ts
    b, length, d = x.shape
    nt = length // tm
    bm = sh.shape[0]
    mod = pl.BlockSpec((1, 1, d), (lambda i, t: (i, 0, 0)) if bm > 1 else (lambda i, t: (0, 0, 0)))
    out_shape = [
        jax.ShapeDtypeStruct((b, length // tq, N_Q_HEADS, KV_WIDTH, tq), BF16),
        jax.ShapeDtypeStruct((b, length, KV_WIDTH), BF16),
        jax.ShapeDtypeStruct((b, N_KV_HEADS, length // tk, V_ROWS, tk), BF16),
        jax.ShapeDtypeStruct((b, length, POOL_WIDTH), F32),
        jax.ShapeDtypeStruct((b, 2, 2, length // 2, FOUR_WIDTH), BF16),
    ]
    out_specs = [
        pl.BlockSpec((1, tm // tq, N_Q_HEADS, KV_WIDTH, tq), lambda i, t: (i, t, 0, 0, 0)),
        pl.BlockSpec((1, tm, KV_WIDTH), lambda i, t: (i, t, 0)),
        pl.BlockSpec((1, N_KV_HEADS, tm // tk, V_ROWS, tk), lambda i, t: (i, 0, t, 0, 0)),
        pl.BlockSpec((1, tm, POOL_WIDTH), lambda i, t: (i, t, 0)),
        pl.BlockSpec((1, 2, 2, tm // 2, FOUR_WIDTH), lambda i, t: (i, 0, 0, t, 0)),
    ]
    in_specs = [
        pl.BlockSpec((1, tm, d), lambda i, t: (i, t, 0)),
        mod, mod,
        _resident((1, d)),
        _resident(wqv.shape), _resident(wr.shape), _resident(hm.shape), _resident(kg.shape),
        pl.BlockSpec((2 * HEAD_DIM, tm), lambda i, t: (0, t)),
        pl.BlockSpec((tm, KV_WIDTH), lambda i, t: (t, 0)),
        pl.BlockSpec((tm, KV_WIDTH), lambda i, t: (t, 0)),
        _resident(ab.shape),
    ]
    return pl.pallas_call(
        functools.partial(_in_proj_kernel, ts=ts),
        out_shape=out_shape, grid=(b, nt), in_specs=in_specs, out_specs=out_specs,
        compiler_params=_params("parallel", "parallel"),
        scratch_shapes=[pltpu.VMEM((2 * FOUR_WIDTH // LANES, ts, LANES), F32)],
        name="in_proj",
    )(x, sh, sc, g, wqv, wr, hm, kg, qtab, kcos, ksin, ab)


def _attn_kernel(*refs, tk, n_ctx, n_lat, ahead):
    if n_lat:
        q_ref, kc_ref, vc_ref, k_ref, v_ref, o_ref, m_sc, acc_sc = refs
    else:
        q_ref, kc_ref, vc_ref, o_ref, m_sc, acc_sc = refs
    tiles, tq = q_ref.shape[1], q_ref.shape[4]
    m_sc[...] = jnp.full(m_sc.shape, -jnp.inf, F32)
    acc_sc[...] = jnp.zeros(acc_sc.shape, F32)

    def kv_block(c):
        if c < n_ctx:
            return kc_ref[0, c * tk:(c + 1) * tk, :], [vc_ref[0, g, c] for g in range(N_KV_HEADS)]
        c -= n_ctx
        return k_ref[0, c * tk:(c + 1) * tk, :], [v_ref[0, g, c] for g in range(N_KV_HEADS)]

    n_blocks = n_ctx + n_lat
    blocks = [kv_block(c) for c in range(n_blocks)]
    items = [(t, c, hd) for t in range(tiles) for c in range(n_blocks) for hd in range(N_Q_HEADS)]
    scores = {}

    def issue(i):
        t, c, hd = items[i]
        scores[i] = jnp.dot(blocks[c][0], q_ref[0, t, hd], preferred_element_type=F32)

    for i in range(min(ahead, len(items))):
        issue(i)
    for i, (t, c, hd) in enumerate(items):
        if i + ahead < len(items):
            issue(i + ahead)
        s = scores.pop(i)
        slot = t * N_Q_HEADS + hd
        m_old = m_sc[slot]
        m_new = jnp.maximum(m_old, jnp.max(s, axis=0, keepdims=True))
        alpha = jnp.exp2(m_old - m_new)
        p = jnp.exp2(s - m_new)
        acc_sc[slot] = alpha * acc_sc[slot] + jnp.dot(blocks[c][1][hd // GQA_GROUP], p.astype(BF16),
                                                      preferred_element_type=F32)
        m_sc[slot] = m_new

    for t in range(tiles):
        for hd in range(N_Q_HEADS):
            slot = t * N_Q_HEADS + hd
            o = acc_sc[slot, 0:HEAD_DIM, :] * (1.0 / acc_sc[slot, HEAD_DIM:HEAD_DIM + 1, :])
            o_ref[0, hd * HEAD_DIM:(hd + 1) * HEAD_DIM, t * tq:(t + 1) * tq] = o.astype(BF16)


def _attention(qt, kc, vc, k=None, v=None, *, tk):
    b, nq, _, _, tq = qt.shape
    length = nq * tq
    tiles = math.gcd(ATTN_TILES, nq)
    n_ctx = kc.shape[1] // tk
    n_lat = 0 if k is None else k.shape[1] // tk
    in_specs = [
        pl.BlockSpec((1, tiles) + qt.shape[2:], lambda i, j: (i, j, 0, 0, 0)),
        pl.BlockSpec((1,) + kc.shape[1:], lambda i, j: (i, 0, 0)),
        pl.BlockSpec((1,) + vc.shape[1:], lambda i, j: (i, 0, 0, 0, 0)),
    ]
    args = [qt, kc, vc]
    if n_lat:
        in_specs += [pl.BlockSpec((1,) + k.shape[1:], lambda i, j: (i, 0, 0)),
                     pl.BlockSpec((1,) + v.shape[1:], lambda i, j: (i, 0, 0, 0, 0))]
        args += [k, v]
    return pl.pallas_call(
        functools.partial(_attn_kernel, tk=tk, n_ctx=n_ctx, n_lat=n_lat, ahead=ATTN_LOOKAHEAD),
        out_shape=jax.ShapeDtypeStruct((b, Q_WIDTH, length), BF16),
        grid=(b, nq // tiles),
        in_specs=in_specs,
        out_specs=pl.BlockSpec((1, Q_WIDTH, tiles * tq), lambda i, j: (i, 0, j)),
        scratch_shapes=[pltpu.VMEM((tiles * N_Q_HEADS, 1, tq), F32),
                        pltpu.VMEM((tiles * N_Q_HEADS, V_ROWS, tq), F32)],
        compiler_params=_params("parallel", "parallel"),
        name="attention",
    )(*args)


def _pool_kernel(u_ref, w_ref, s_ref, o_ref, pad_ref, *, chunk):
    length = u_ref.shape[1]
    width = u_ref.shape[2]
    hw = POOL_HALO
    pad_ref[0:hw, :] = jnp.zeros((hw, width), F32)
    pad_ref[hw:hw + length, :] = u_ref[0]
    pad_ref[hw + length:, :] = jnp.zeros((hw, width), F32)
    assert 2 * POOL_GROUP == LANES
    first = lax.broadcasted_iota(jnp.int32, (chunk, LANES), 1) < POOL_GROUP
    row = lax.broadcasted_iota(jnp.int32, (chunk, LANES), 0)
    for c in range(length // chunk):
        r0 = c * chunk
        clipped = r0 < hw or r0 + chunk + hw > length
        cols = []
        for col in range(width // LANES):
            wins = POOL_WINDOWS[2 * col:2 * col + 2]
            cl = slice(col * LANES, (col + 1) * LANES)
            if wins[1] < hw:
                sums = []
                acc = None
                lo, hi = hw, hw
                for w in wins:
                    for off in list(range(hw - w // 2, lo)) + list(range(hi, hw + w // 2)):
                        piece = pad_ref[r0 + off:r0 + off + chunk, cl]
                        acc = piece if acc is None else acc + piece
                    lo, hi = hw - w // 2, hw + w // 2
                    sums.append(acc)
            else:
                rows = chunk + 2 * hw
                run = pad_ref[r0:r0 + rows, cl]
                w = 1
                while w < wins[0]:
                    run = run + pltpu.roll(run, rows - w, 0)
                    w *= 2
                assert (w, 2 * w) == tuple(wins) and 2 * w == 2 * hw
                sums = [pltpu.roll(run, rows - (hw - w // 2), 0)[0:chunk],
                        run[0:chunk] + run[w:w + chunk]]
            s = jnp.where(first, sums[0], sums[1])
            if clipped:
                win = jnp.where(first, wins[0], wins[1])
                t = r0 + row
                cnt = jnp.minimum(t + win - win // 2, length) - jnp.maximum(t - win // 2, 0)
                mean = s / cnt.astype(F32)
            else:
                mean = s * jnp.where(first, 1.0 / wins[0], 1.0 / wins[1])
            cols.append(mean - pad_ref[r0 + hw:r0 + hw + chunk, cl])
        d = jnp.concatenate(cols, axis=1)
        y = jnp.dot(d.astype(BF16), w_ref[...], preferred_element_type=F32) * s_ref[...]
        o_ref[0, r0:r0 + chunk, :] = y.astype(BF16)


def _pool(u, w_bd, scale, *, chunk):
    b, length, width = u.shape
    return pl.pallas_call(
        functools.partial(_pool_kernel, chunk=chunk),
        out_shape=jax.ShapeDtypeStruct((b, length, width), BF16),
        grid=(b,),
        in_specs=[pl.BlockSpec((1, length, width), lambda i: (i, 0, 0)),
                  _resident(w_bd.shape), _resident(scale.shape)],
        out_specs=pl.BlockSpec((1, length, width), lambda i: (i, 0, 0)),
        scratch_shapes=[pltpu.VMEM((length + 2 * POOL_HALO, width), F32)],
        compiler_params=_params("parallel"),
        name="pool_mix",
    )(u, w_bd, scale)


def _dft_kernel(te_ref, to_ref, v_ref, o_ref):
    e = jnp.dot(te_ref[...], v_ref[0, 0], preferred_element_type=F32)
    o = jnp.dot(to_ref[...], v_ref[0, 1], preferred_element_type=F32)
    o_ref[0, 0] = (e + o).astype(BF16)
    o_ref[0, 1] = (e - o).astype(BF16)


def _dft(tables, vf, *, tl):
    b, _, _, half, width = vf.shape
    v2 = vf.reshape(b, 2, 2 * half, width)
    tab = pl.BlockSpec((tl, 2 * half), lambda i, j: (i, 0))
    y = pl.pallas_call(
        _dft_kernel,
        out_shape=jax.ShapeDtypeStruct((b, 2, half, width), BF16),
        grid=(half // tl, b),
        in_specs=[tab, tab, pl.BlockSpec((1, 2, 2 * half, width), lambda i, j: (j, 0, 0, 0))],
        out_specs=pl.BlockSpec((1, 2, tl, width), lambda i, j: (j, 0, i, 0)),
        compiler_params=_params("parallel", "parallel"),
        name="seq_dft",
    )(tables[0], tables[1], v2)
    return y.reshape(b, 2 * half, width)


def _dft_tables(length):
    half = length // 2
    nc = 64
    na = half // nc
    w = 2.0 * math.pi / length
    k = jnp.arange(half, dtype=jnp.int32)[:, None]
    a = jnp.arange(na, dtype=jnp.int32)[None, :]
    c = jnp.arange(nc, dtype=jnp.int32)[None, :]
    alpha = ((k * (2 * nc * a)) % length).astype(F32) * w
    ca, sa = jnp.cos(alpha)[:, :, None], jnp.sin(alpha)[:, :, None]
    tables = []
    for par in range(2):
        beta = ((k * (2 * c + par)) % length).astype(F32) * w
        cb, sb = jnp.cos(beta)[:, None, :], jnp.sin(beta)[:, None, :]
        cs = (ca * cb - sa * sb).reshape(half, half)
        sn = (sa * cb + ca * sb).reshape(half, half)
        tables.append((jnp.concatenate([cs, sn], axis=1) * (length ** -0.5)).astype(BF16))
    return tables


def _out_ffn_kernel(x_ref, ot_ref, yp_ref, yf_ref, ga1_ref, sh2_ref, sc2_ref, ga2_ref, g_ref,
                    wo_ref, wgu_ref, wd_ref, o_ref, *, fc):
    d_ff = wd_ref.shape[0]
    mix = lax.dot_general(ot_ref[0], wo_ref[0:Q_WIDTH, :], (((0,), (0,)), ((), ())),
                          preferred_element_type=F32)
    mix += jnp.dot(yp_ref[0], wo_ref[Q_WIDTH:Q_WIDTH + POOL_WIDTH, :], preferred_element_type=F32)
    mix += jnp.dot(yf_ref[0], wo_ref[Q_WIDTH + POOL_WIDTH:, :], preferred_element_type=F32)
    x1 = x_ref[0] + ga1_ref[0] * mix
    f = _rms(x1) * g_ref[...]
    fb = (f * (1.0 + sc2_ref[0]) + sh2_ref[0]).astype(BF16)
    acc = jnp.zeros(x1.shape, F32)
    for c in range(d_ff // fc):
        gate = jnp.dot(fb, wgu_ref[:, c * fc:(c + 1) * fc], preferred_element_type=F32)
        up = jnp.dot(fb, wgu_ref[:, d_ff + c * fc:d_ff + (c + 1) * fc], preferred_element_type=F32)
        hc = (gate / (1.0 + jnp.exp(-gate)) * up).astype(BF16)
        acc += jnp.dot(hc, wd_ref[c * fc:(c + 1) * fc, :], preferred_element_type=F32)
    o_ref[0] = x1 + ga2_ref[0] * acc


def _out_ffn(x, ot, yp, yf, ga1, sh2, sc2, ga2, g, wo, wgu, wd, *, tm, fc):
    b, length, d = x.shape
    bm = ga1.shape[0]
    mod = pl.BlockSpec((1, 1, d), (lambda i, t: (i, 0, 0)) if bm > 1 else (lambda i, t: (0, 0, 0)))
    tok = lambda w: pl.BlockSpec((1, tm, w), lambda i, t: (i, t, 0))
    return pl.pallas_call(
        functools.partial(_out_ffn_kernel, fc=fc),
        out_shape=jax.ShapeDtypeStruct(x.shape, F32),
        grid=(b, length // tm),
        in_specs=[tok(d), pl.BlockSpec((1, Q_WIDTH, tm), lambda i, t: (i, 0, t)),
                  tok(POOL_WIDTH), tok(FOUR_WIDTH), mod, mod, mod, mod, _resident((1, d)),
                  _resident(wo.shape), _resident(wgu.shape), _resident(wd.shape)],
        out_specs=tok(d),
        compiler_params=_params("parallel", "parallel"),
        name="out_ffn",
    )(x, ot, yp, yf, ga1, sh2, sc2, ga2, g, wo, wgu, wd)


def _rope_cos_sin(length, rotate):
    if not rotate:
        return jnp.ones((length, 2, N_FREQ), F32), jnp.zeros((length, 2, N_FREQ), F32)
    rows = length // GRID_W
    row_ids = jnp.repeat(jnp.arange(rows), GRID_W).astype(F32)
    col_ids = jnp.tile(jnp.arange(GRID_W), rows).astype(F32)
    freqs = ROPE_THETA ** (-jnp.arange(N_FREQ, dtype=F32) / N_FREQ)
    ang = jnp.stack([row_ids[:, None] * freqs, col_ids[:, None] * freqs], axis=1)
    return jnp.cos(ang), jnp.sin(ang)


def _q_table(cos, sin, q_gain):
    gq = (q_gain * (HEAD_DIM ** -0.5 * math.log2(math.e))).reshape(2, 2, N_FREQ)
    c = jnp.transpose(cos, (1, 2, 0))
    s = jnp.transpose(sin, (1, 2, 0))
    g1, g2 = gq[:, 0, :, None], gq[:, 1, :, None]
    tab = jnp.stack([g1 * c, g2 * s, g2 * c, g1 * s], axis=1)
    return tab.reshape(2 * HEAD_DIM, -1)


def _k_tables(cos, sin):
    c = jnp.concatenate([cos, cos], axis=2).reshape(cos.shape[0], HEAD_DIM)
    s = jnp.concatenate([-sin, sin], axis=2).reshape(sin.shape[0], HEAD_DIM)
    return jnp.tile(c, (1, N_KV_HEADS)), jnp.tile(s, (1, N_KV_HEADS))


def kernel(x, c, ctx, c_ctx, w_ada, b_ada, g_mix, g_ffn, w_in, q_gain, k_gain, w_pool, pool_scale,
           w_four, w_out, w_gate_up, w_down):
    b, s, d = x.shape
    cl = ctx.shape[1]
    depth = w_ada.shape[0]

    tm_x, tm_c = 512, cl
    tmi_x, tmi_c = 1024, cl
    ts = 256
    tl_x, tl_c = 512, cl // 2
    pool_chunk = 256
    fc = 256

    n_rows = -(-(b + 1) // 8) * 8
    cvec = jnp.zeros((n_rows, d), F32).at[:b].set(c).at[b].set(c_ctx)
    mod = _ada(cvec, w_ada, b_ada)
    gr, gi = _four_weights(w_four)

    cos_x, sin_x = _rope_cos_sin(s, True)
    cos_c, sin_c = _rope_cos_sin(cl, False)
    kt_x = _k_tables(cos_x, sin_x)
    kt_c = _k_tables(cos_c, sin_c)
    dtab_x = _dft_tables(s)
    dtab_c = _dft_tables(cl)
    hm = _block_diag(jnp.full((N_KV_HEADS, HEAD_DIM, HEAD_DIM), 1.0 / HEAD_DIM, F32)).astype(BF16)

    q_end, k_end = Q_WIDTH, Q_WIDTH + KV_WIDTH
    kv_end = k_end + KV_WIDTH
    for i in range(depth):
        update_ctx = i < depth - 1
        wi = w_in[i]
        wqv = jnp.concatenate([wi[:, :q_end], wi[:, k_end:kv_end]], axis=1).T.astype(BF16)
        wr = jnp.concatenate([wi[:, q_end:k_end], wi[:, kv_end:]], axis=1).astype(BF16)
        ab = jnp.concatenate([_block_diag(gr[i]), _block_diag(gi[i])], axis=1).astype(BF16)
        wp = _block_diag(w_pool[i]).astype(BF16)
        ps = pool_scale[i].reshape(1, -1)
        kg = jnp.tile(k_gain[i], N_KV_HEADS).reshape(1, -1)
        wo = w_out[i].astype(BF16)
        wgu = w_gate_up[i].astype(BF16)
        wd = w_down[i].astype(BF16)
        gm = g_mix[i].reshape(1, d)
        gf = g_ffn[i].reshape(1, d)
        mx = [mod[i, :b, j * d:(j + 1) * d].reshape(b, 1, d) for j in range(6)]
        mc = [mod[i, b, j * d:(j + 1) * d].reshape(1, 1, d) for j in range(6)]

        def in_proj(xx, m, cos, sin, kt, tm):
            return _in_proj(xx, m[0], m[1], gm, wqv, wr, hm, kg, _q_table(cos, sin, q_gain[i]),
                            kt[0], kt[1], ab, tm=tm, ts=ts)

        def finish(xx, ot, up, vf, m, dtab, tm, tl):
            yp = _pool(up, wp, ps, chunk=pool_chunk)
            yf = _dft(dtab, vf, tl=tl)
            return _out_ffn(xx, ot, yp, yf, m[2], m[3], m[4], m[5], gf, wo, wgu, wd, tm=tm, fc=fc)

        qt_c, k_c, vt_c, up_c, vf_c = in_proj(ctx, mc, cos_c, sin_c, kt_c, tmi_c)
        qt_x, k_x, vt_x, up_x, vf_x = in_proj(x, mx, cos_x, sin_x, kt_x, tmi_x)
        ot_x = _attention(qt_x, k_c, vt_c, k_x, vt_x, tk=ts)
        x = finish(x, ot_x, up_x, vf_x, mx, dtab_x, tm_x, tl_x)
        if update_ctx:
            ot_c = _attention(qt_c, k_c, vt_c, tk=ts)
            ctx = finish(ctx, ot_c, up_c, vf_c, mc, dtab_c, tm_c, tl_c)
    return x
```

```python
import functools
import math

import jax
import jax.numpy as jnp
from jax import lax
from jax.experimental import pallas as pl
from jax.experimental.pallas import tpu as pltpu

F32 = jnp.float32
BF16 = jnp.bfloat16

GRID_W = 64
HEAD_DIM = 64
N_Q_HEADS = 8
N_KV_HEADS = 2
GQA_GROUP = N_Q_HEADS // N_KV_HEADS
Q_WIDTH = N_Q_HEADS * HEAD_DIM
KV_WIDTH = N_KV_HEADS * HEAD_DIM
POOL_WINDOWS = (2, 4, 8, 16)
POOL_WIDTH = 256
POOL_GROUP = 64
FOUR_WIDTH = 256
N_FREQ = HEAD_DIM // 4
ROPE_THETA = 10000.0
EPS = 1e-6
POOL_HALO = max(POOL_WINDOWS) // 2
LANES = 128
V_ROWS = HEAD_DIM + 16
ATTN_LOOKAHEAD = 5

VMEM_LIMIT_BYTES = 56 * 1024 * 1024


def _params(*sem):
    return pltpu.CompilerParams(dimension_semantics=sem, vmem_limit_bytes=VMEM_LIMIT_BYTES)


def _resident(shape):
    zeros = (0,) * len(shape)
    return pl.BlockSpec(shape, lambda *_: zeros, pipeline_mode=pl.Buffered(1))


def _rms(x):
    return x * lax.rsqrt(jnp.mean(x * x, axis=-1, keepdims=True) + EPS)


def _ada_kernel(c_ref, w_ref, b_ref, o_ref):
    c = c_ref[...]
    sc = c / (1.0 + jnp.exp(-c))
    o_ref[0] = jnp.dot(sc, w_ref[0], preferred_element_type=F32,
                       precision=lax.Precision.HIGHEST) + b_ref[0]


def _ada(cvec, w_ada, b_ada):
    depth, d, n = w_ada.shape
    r = cvec.shape[0]
    tn = 1536
    return pl.pallas_call(
        _ada_kernel,
        out_shape=jax.ShapeDtypeStruct((depth, r, n), F32),
        grid=(depth, n // tn),
        in_specs=[pl.BlockSpec((r, d), lambda i, j: (0, 0)),
                  pl.BlockSpec((1, d, tn), lambda i, j: (i, 0, j)),
                  pl.BlockSpec((1, 1, tn), lambda i, j: (i, 0, j))],
        out_specs=pl.BlockSpec((1, r, tn), lambda i, j: (i, 0, j)),
        compiler_params=_params("arbitrary", "arbitrary"),
        name="ada_mod",
    )(cvec, w_ada, b_ada.reshape(depth, 1, n))


def _four_w_kernel(c_ref, s_ref, w_ref, gr_ref, gi_ref):
    for g in range(w_ref.shape[1]):
        w = w_ref[0, g]
        gr_ref[0, g] = jnp.dot(c_ref[...], w, preferred_element_type=F32, precision=lax.Precision.HIGHEST)
        gi_ref[0, g] = -jnp.dot(s_ref[...], w, preferred_element_type=F32, precision=lax.Precision.HIGHEST)


def _four_weights(w_four):
    depth, ng, gc, _ = w_four.shape
    c = jnp.arange(gc, dtype=jnp.int32)
    ang = ((c[:, None] * c[None, :]) % gc).astype(F32) * (2.0 * math.pi / gc)
    scale = gc ** -0.5
    spec = pl.BlockSpec((1, ng, gc, gc), lambda i: (i, 0, 0, 0))
    tab = pl.BlockSpec((gc, gc), lambda i: (0, 0))
    gr, gi = pl.pallas_call(
        _four_w_kernel,
        out_shape=[jax.ShapeDtypeStruct(w_four.shape, F32)] * 2,
        grid=(depth,),
        in_specs=[tab, tab, spec],
        out_specs=[spec, spec],
        compiler_params=_params("arbitrary"),
        name="four_weights",
    )(jnp.cos(ang) * scale, jnp.sin(ang) * scale, w_four)
    return gr, gi


def _block_diag(blocks):
    g, a, b = blocks.shape
    eye = jnp.eye(g, dtype=blocks.dtype)
    return (eye[:, None, :, None] * blocks[:, :, None, :]).reshape(g * a, g * b)


def _in_proj_kernel(x_ref, sh_ref, sc_ref, g_ref, wqv_ref, wr_ref, hm_ref, kg_ref, qtab_ref,
                    kcos_ref, ksin_ref, ab_ref, q_ref, k_ref, v_ref, up_ref, vf_ref, vf_sc, *, ts):
    tm = x_ref.shape[1]

    def project(j):
        rows = slice(j * ts, (j + 1) * ts)
        h = _rms(x_ref[0, rows, :]) * g_ref[...]
        h = h * (1.0 + sc_ref[0]) + sh_ref[0]
        hb = h.astype(BF16)
        zt = lax.dot_general(wqv_ref[...], hb, (((1,), (1,)), ((), ())), preferred_element_type=F32)
        zr = jnp.dot(hb, wr_ref[...], preferred_element_type=F32)
        return zt, zr

    def emit(j, zt, zr):
        rows = slice(j * ts, (j + 1) * ts)
        tab = qtab_ref[:, rows]
        for hd in range(N_Q_HEADS):
            qh = zt[hd * HEAD_DIM:(hd + 1) * HEAD_DIM, :]
            ss = jnp.sum(qh * qh, axis=0, keepdims=True) * (1.0 / HEAD_DIM)
            qn = qh * lax.rsqrt(ss + EPS)
            parts = []
            for seg in range(2):
                x1 = qn[seg * 32:seg * 32 + N_FREQ]
                x2 = qn[seg * 32 + N_FREQ:seg * 32 + 2 * N_FREQ]
                t0 = seg * 64
                g1c, g2s = tab[t0:t0 + 16], tab[t0 + 16:t0 + 32]
                g2c, g1s = tab[t0 + 32:t0 + 48], tab[t0 + 48:t0 + 64]
                parts += [x1 * g1c - x2 * g2s, x2 * g2c + x1 * g1s]
            g = hd // GQA_GROUP
            o = 1 - g
            q_ref[0, j, hd, g * HEAD_DIM:(g + 1) * HEAD_DIM, :] = jnp.concatenate(parts, axis=0).astype(BF16)
            q_ref[0, j, hd, o * HEAD_DIM:(o + 1) * HEAD_DIM, :] = jnp.zeros((HEAD_DIM, ts), BF16)

        vt = zt[Q_WIDTH:Q_WIDTH + KV_WIDTH].astype(BF16)
        for g in range(N_KV_HEADS):
            v_ref[0, g, j, 0:HEAD_DIM, :] = vt[g * HEAD_DIM:(g + 1) * HEAD_DIM, :]
            v_ref[0, g, j, HEAD_DIM:, :] = jnp.ones((V_ROWS - HEAD_DIM, ts), BF16)

        kz = zr[:, :KV_WIDTH]
        ssk = jnp.dot((kz * kz).astype(BF16), hm_ref[...], preferred_element_type=F32)
        kn = kz * lax.rsqrt(ssk + EPS) * kg_ref[...]
        lane = lax.broadcasted_iota(jnp.int32, kn.shape, 1)
        rot = jnp.where((lane % 32) < N_FREQ,
                        pltpu.roll(kn, KV_WIDTH - N_FREQ, 1), pltpu.roll(kn, N_FREQ, 1))
        k_ref[0, rows, :] = (kn * kcos_ref[rows, :] + rot * ksin_ref[rows, :]).astype(BF16)

        up_ref[0, rows, :] = zr[:, KV_WIDTH:KV_WIDTH + POOL_WIDTH]
        uf = zr[:, KV_WIDTH + POOL_WIDTH:].astype(BF16)
        v2 = jnp.dot(uf, ab_ref[...], preferred_element_type=F32)
        n_chunks = 2 * FOUR_WIDTH // LANES
        per_part = FOUR_WIDTH // LANES
        half = slice(j * ts // 2, (j + 1) * ts // 2)
        for ch in range(n_chunks):
            vf_sc[ch] = v2[:, ch * LANES:(ch + 1) * LANES]
        for par in range(2):
            for ch in range(n_chunks):
                picked = vf_sc[ch, pl.ds(par, ts // 2, stride=2), :]
                part, col = divmod(ch, per_part)
                vf_ref[0, par, part, half, col * LANES:(col + 1) * LANES] = picked.astype(BF16)

    pending = project(0)
    for j in range(tm // ts):
        following = project(j + 1) if (j + 1) * ts < tm else None
        emit(j, *pending)
        pending = following


def _in_proj(x, sh, sc, g, wqv, wr, hm, kg, qtab, kcos, ksin, ab, *, tm, ts):
    tq = tk = ts
    b, length, d = x.shape
    nt = length // tm
    bm = sh.shape[0]
    mod = pl.BlockSpec((1, 1, d), (lambda i, t: (i, 0, 0)) if bm > 1 else (lambda i, t: (0, 0, 0)))
    out_shape = [
        jax.ShapeDtypeStruct((b, length // tq, N_Q_HEADS, KV_WIDTH, tq), BF16),
        jax.ShapeDtypeStruct((b, length, KV_WIDTH), BF16),
        jax.ShapeDtypeStruct((b, N_KV_HEADS, length // tk, V_ROWS, tk), BF16),
        jax.ShapeDtypeStruct((b, length, POOL_WIDTH), F32),
        jax.ShapeDtypeStruct((b, 2, 2, length // 2, FOUR_WIDTH), BF16),
    ]
    out_specs = [
        pl.BlockSpec((1, tm // tq, N_Q_HEADS, KV_WIDTH, tq), lambda i, t: (i, t, 0, 0, 0)),
        pl.BlockSpec((1, tm, KV_WIDTH), lambda i, t: (i, t, 0)),
        pl.BlockSpec((1, N_KV_HEADS, tm // tk, V_ROWS, tk), lambda i, t: (i, 0, t, 0, 0)),
        pl.BlockSpec((1, tm, POOL_WIDTH), lambda i, t: (i, t, 0)),
        pl.BlockSpec((1, 2, 2, tm // 2, FOUR_WIDTH), lambda i, t: (i, 0, 0, t, 0)),
    ]
    in_specs = [
        pl.BlockSpec((1, tm, d), lambda i, t: (i, t, 0)),
        mod, mod,
        _resident((1, d)),
        _resident(wqv.shape), _resident(wr.shape), _resident(hm.shape), _resident(kg.shape),
        pl.BlockSpec((2 * HEAD_DIM, tm), lambda i, t: (0, t)),
        pl.BlockSpec((tm, KV_WIDTH), lambda i, t: (t, 0)),
        pl.BlockSpec((tm, KV_WIDTH), lambda i, t: (t, 0)),
        _resident(ab.shape),
    ]
    return pl.pallas_call(
        functools.partial(_in_proj_kernel, ts=ts),
        out_shape=out_shape, grid=(b, nt), in_specs=in_specs, out_specs=out_specs,
        compiler_params=_params("parallel", "parallel"),
        scratch_shapes=[pltpu.VMEM((2 * FOUR_WIDTH // LANES, ts, LANES), F32)],
        name="in_proj",
    )(x, sh, sc, g, wqv, wr, hm, kg, qtab, kcos, ksin, ab)


def _attn_kernel(*refs, tk, n_ctx, n_lat, ahead):
    if n_lat:
        q_ref, kc_ref, vc_ref, k_ref, v_ref, o_ref, m_sc, acc_sc = refs
    else:
        q_ref, kc_ref, vc_ref, o_ref, m_sc, acc_sc = refs
    m_sc[...] = jnp.full(m_sc.shape, -jnp.inf, F32)
    acc_sc[...] = jnp.zeros(acc_sc.shape, F32)

    def kv_block(c):
        if c < n_ctx:
            return kc_ref[0, c * tk:(c + 1) * tk, :], [vc_ref[0, g, c] for g in range(N_KV_HEADS)]
        c -= n_ctx
        return k_ref[0, c * tk:(c + 1) * tk, :], [v_ref[0, g, c] for g in range(N_KV_HEADS)]

    n_blocks = n_ctx + n_lat
    blocks = [kv_block(c) for c in range(n_blocks)]
    items = [(c, hd) for c in range(n_blocks) for hd in range(N_Q_HEADS)]
    scores = {}

    def issue(i):
        c, hd = items[i]
        scores[i] = jnp.dot(blocks[c][0], q_ref[0, 0, hd], preferred_element_type=F32)

    for i in range(min(ahead, len(items))):
        issue(i)
    for i, (c, hd) in enumerate(items):
        if i + ahead < len(items):
            issue(i + ahead)
        s = scores.pop(i)
        m_old = m_sc[hd]
        m_new = jnp.maximum(m_old, jnp.max(s, axis=0, keepdims=True))
        alpha = jnp.exp2(m_old - m_new)
        p = jnp.exp2(s - m_new)
        acc_sc[hd] = alpha * acc_sc[hd] + jnp.dot(blocks[c][1][hd // GQA_GROUP], p.astype(BF16),
                                                  preferred_element_type=F32)
        m_sc[hd] = m_new

    for hd in range(N_Q_HEADS):
        o = acc_sc[hd, 0:HEAD_DIM, :] * (1.0 / acc_sc[hd, HEAD_DIM:HEAD_DIM + 1, :])
        o_ref[0, hd * HEAD_DIM:(hd + 1) * HEAD_DIM, :] = o.astype(BF16)


def _attention(qt, kc, vc, k=None, v=None, *, tk):
    b, nq, _, _, tq = qt.shape
    length = nq * tq
    n_ctx = kc.shape[1] // tk
    n_lat = 0 if k is None else k.shape[1] // tk
    in_specs = [
        pl.BlockSpec((1, 1) + qt.shape[2:], lambda i, j: (i, j, 0, 0, 0)),
        pl.BlockSpec((1,) + kc.shape[1:], lambda i, j: (i, 0, 0)),
        pl.BlockSpec((1,) + vc.shape[1:], lambda i, j: (i, 0, 0, 0, 0)),
    ]
    args = [qt, kc, vc]
    if n_lat:
        in_specs += [pl.BlockSpec((1,) + k.shape[1:], lambda i, j: (i, 0, 0)),
                     pl.BlockSpec((1,) + v.shape[1:], lambda i, j: (i, 0, 0, 0, 0))]
        args += [k, v]
    return pl.pallas_call(
        functools.partial(_attn_kernel, tk=tk, n_ctx=n_ctx, n_lat=n_lat, ahead=ATTN_LOOKAHEAD),
        out_shape=jax.ShapeDtypeStruct((b, Q_WIDTH, length), BF16),
        grid=(b, nq),
        in_specs=in_specs,
        out_specs=pl.BlockSpec((1, Q_WIDTH, tq), lambda i, j: (i, 0, j)),
        scratch_shapes=[pltpu.VMEM((N_Q_HEADS, 1, tq), F32), pltpu.VMEM((N_Q_HEADS, V_ROWS, tq), F32)],
        compiler_params=_params("parallel", "parallel"),
        name="attention",
    )(*args)


def _pool_kernel(u_ref, w_ref, s_ref, o_ref, pad_ref, *, chunk):
    length = u_ref.shape[1]
    width = u_ref.shape[2]
    hw = POOL_HALO
    pad_ref[0:hw, :] = jnp.zeros((hw, width), F32)
    pad_ref[hw:hw + length, :] = u_ref[0]
    pad_ref[hw + length:, :] = jnp.zeros((hw, width), F32)
    assert 2 * POOL_GROUP == LANES
    first = lax.broadcasted_iota(jnp.int32, (chunk, LANES), 1) < POOL_GROUP
    row = lax.broadcasted_iota(jnp.int32, (chunk, LANES), 0)
    for c in range(length // chunk):
        r0 = c * chunk
        clipped = r0 < hw or r0 + chunk + hw > length
        cols = []
        for col in range(width // LANES):
            wins = POOL_WINDOWS[2 * col:2 * col + 2]
            cl = slice(col * LANES, (col + 1) * LANES)
            if wins[1] < hw:
                sums = []
                acc = None
                lo, hi = hw, hw
                for w in wins:
                    for off in list(range(hw - w // 2, lo)) + list(range(hi, hw + w // 2)):
                        piece = pad_ref[r0 + off:r0 + off + chunk, cl]
                        acc = piece if acc is None else acc + piece
                    lo, hi = hw - w // 2, hw + w // 2
                    sums.append(acc)
            else:
                rows = chunk + 2 * hw
                run = pad_ref[r0:r0 + rows, cl]
                w = 1
                while w < wins[0]:
                    run = run + pltpu.roll(run, rows - w, 0)
                    w *= 2
                assert (w, 2 * w) == tuple(wins) and 2 * w == 2 * hw
                sums = [pltpu.roll(run, rows - (hw - w // 2), 0)[0:chunk],
                        run[0:chunk] + run[w:w + chunk]]
            s = jnp.where(first, sums[0], sums[1])
            if clipped:
                win = jnp.where(first, wins[0], wins[1])
                t = r0 + row
                cnt = jnp.minimum(t + win - win // 2, length) - jnp.maximum(t - win // 2, 0)
                mean = s / cnt.astype(F32)
            else:
                mean = s * jnp.where(first, 1.0 / wins[0], 1.0 / wins[1])
            cols.append(mean - pad_ref[r0 + hw:r0 + hw + chunk, cl])
        d = jnp.concatenate(cols, axis=1)
        y = jnp.dot(d.astype(BF16), w_ref[...], preferred_element_type=F32) * s_ref[...]
        o_ref[0, r0:r0 + chunk, :] = y.astype(BF16)


def _pool(u, w_bd, scale, *, chunk):
    b, length, width = u.shape
    return pl.pallas_call(
        functools.partial(_pool_kernel, chunk=chunk),
        out_shape=jax.ShapeDtypeStruct((b, length, width), BF16),
        grid=(b,),
        in_specs=[pl.BlockSpec((1, length, width), lambda i: (i, 0, 0)),
                  _resident(w_bd.shape), _resident(scale.shape)],
        out_specs=pl.BlockSpec((1, length, width), lambda i: (i, 0, 0)),
        scratch_shapes=[pltpu.VMEM((length + 2 * POOL_HALO, width), F32)],
        compiler_params=_params("parallel"),
        name="pool_mix",
    )(u, w_bd, scale)


def _dft_kernel(te_ref, to_ref, v_ref, o_ref):
    e = jnp.dot(te_ref[...], v_ref[0, 0], preferred_element_type=F32)
    o = jnp.dot(to_ref[...], v_ref[0, 1], preferred_element_type=F32)
    o_ref[0, 0] = (e + o).astype(BF16)
    o_ref[0, 1] = (e - o).astype(BF16)


def _dft(tables, vf, *, tl):
    b, _, _, half, width = vf.shape
    v2 = vf.reshape(b, 2, 2 * half, width)
    tab = pl.BlockSpec((tl, 2 * half), lambda i, j: (i, 0))
    y = pl.pallas_call(
        _dft_kernel,
        out_shape=jax.ShapeDtypeStruct((b, 2, half, width), BF16),
        grid=(half // tl, b),
        in_specs=[tab, tab, pl.BlockSpec((1, 2, 2 * half, width), lambda i, j: (j, 0, 0, 0))],
        out_specs=pl.BlockSpec((1, 2, tl, width), lambda i, j: (j, 0, i, 0)),
        compiler_params=_params("parallel", "parallel"),
        name="seq_dft",
    )(tables[0], tables[1], v2)
    return y.reshape(b, 2 * half, width)


def _dft_tables(length):
    half = length // 2
    nc = 64
    na = half // nc
    w = 2.0 * math.pi / length
    k = jnp.arange(half, dtype=jnp.int32)[:, None]
    a = jnp.arange(na, dtype=jnp.int32)[None, :]
    c = jnp.arange(nc, dtype=jnp.int32)[None, :]
    alpha = ((k * (2 * nc * a)) % length).astype(F32) * w
    ca, sa = jnp.cos(alpha)[:, :, None], jnp.sin(alpha)[:, :, None]
    tables = []
    for par in range(2):
        beta = ((k * (2 * c + par)) % length).astype(F32) * w
        cb, sb = jnp.cos(beta)[:, None, :], jnp.sin(beta)[:, None, :]
        cs = (ca * cb - sa * sb).reshape(half, half)
        sn = (sa * cb + ca * sb).reshape(half, half)
        tables.append((jnp.concatenate([cs, sn], axis=1) * (length ** -0.5)).astype(BF16))
    return tables


def _out_ffn_kernel(x_ref, ot_ref, yp_ref, yf_ref, ga1_ref, sh2_ref, sc2_ref, ga2_ref, g_ref,
                    wo_ref, wgu_ref, wd_ref, o_ref, *, fc):
    d_ff = wd_ref.shape[0]
    mix = lax.dot_general(ot_ref[0], wo_ref[0:Q_WIDTH, :], (((0,), (0,)), ((), ())),
                          preferred_element_type=F32)
    mix += jnp.dot(yp_ref[0], wo_ref[Q_WIDTH:Q_WIDTH + POOL_WIDTH, :], preferred_element_type=F32)
    mix += jnp.dot(yf_ref[0], wo_ref[Q_WIDTH + POOL_WIDTH:, :], preferred_element_type=F32)
    x1 = x_ref[0] + ga1_ref[0] * mix
    f = _rms(x1) * g_ref[...]
    fb = (f * (1.0 + sc2_ref[0]) + sh2_ref[0]).astype(BF16)
    acc = jnp.zeros(x1.shape, F32)
    for c in range(d_ff // fc):
        gate = jnp.dot(fb, wgu_ref[:, c * fc:(c + 1) * fc], preferred_element_type=F32)
        up = jnp.dot(fb, wgu_ref[:, d_ff + c * fc:d_ff + (c + 1) * fc], preferred_element_type=F32)
        hc = (gate / (1.0 + jnp.exp(-gate)) * up).astype(BF16)
        acc += jnp.dot(hc, wd_ref[c * fc:(c + 1) * fc, :], preferred_element_type=F32)
    o_ref[0] = x1 + ga2_ref[0] * acc


def _out_ffn(x, ot, yp, yf, ga1, sh2, sc2, ga2, g, wo, wgu, wd, *, tm, fc):
    b, length, d = x.shape
    bm = ga1.shape[0]
    mod = pl.BlockSpec((1, 1, d), (lambda i, t: (i, 0, 0)) if bm > 1 else (lambda i, t: (0, 0, 0)))
    tok = lambda w: pl.BlockSpec((1, tm, w), lambda i, t: (i, t, 0))
    return pl.pallas_call(
        functools.partial(_out_ffn_kernel, fc=fc),
        out_shape=jax.ShapeDtypeStruct(x.shape, F32),
        grid=(b, length // tm),
        in_specs=[tok(d), pl.BlockSpec((1, Q_WIDTH, tm), lambda i, t: (i, 0, t)),
                  tok(POOL_WIDTH), tok(FOUR_WIDTH), mod, mod, mod, mod, _resident((1, d)),
                  _resident(wo.shape), _resident(wgu.shape), _resident(wd.shape)],
        out_specs=tok(d),
        compiler_params=_params("parallel", "parallel"),
        name="out_ffn",
    )(x, ot, yp, yf, ga1, sh2, sc2, ga2, g, wo, wgu, wd)


def _rope_cos_sin(length, rotate):
    if not rotate:
        return jnp.ones((length, 2, N_FREQ), F32), jnp.zeros((length, 2, N_FREQ), F32)
    rows = length // GRID_W
    row_ids = jnp.repeat(jnp.arange(rows), GRID_W).astype(F32)
    col_ids = jnp.tile(jnp.arange(GRID_W), rows).astype(F32)
    freqs = ROPE_THETA ** (-jnp.arange(N_FREQ, dtype=F32) / N_FREQ)
    ang = jnp.stack([row_ids[:, None] * freqs, col_ids[:, None] * freqs], axis=1)
    return jnp.cos(ang), jnp.sin(ang)


def _q_table(cos, sin, q_gain):
    gq = (q_gain * (HEAD_DIM ** -0.5 * math.log2(math.e))).reshape(2, 2, N_FREQ)
    c = jnp.transpose(cos, (1, 2, 0))
    s = jnp.transpose(sin, (1, 2, 0))
    g1, g2 = gq[:, 0, :, None], gq[:, 1, :, None]
    tab = jnp.stack([g1 * c, g2 * s, g2 * c, g1 * s], axis=1)
    return tab.reshape(2 * HEAD_DIM, -1)


def _k_tables(cos, sin):
    c = jnp.concatenate([cos, cos], axis=2).reshape(cos.shape[0], HEAD_DIM)
    s = jnp.concatenate([-sin, sin], axis=2).reshape(sin.shape[0], HEAD_DIM)
    return jnp.tile(c, (1, N_KV_HEADS)), jnp.tile(s, (1, N_KV_HEADS))


def kernel(x, c, ctx, c_ctx, w_ada, b_ada, g_mix, g_ffn, w_in, q_gain, k_gain, w_pool, pool_scale,
           w_four, w_out, w_gate_up, w_down):
    b, s, d = x.shape
    cl = ctx.shape[1]
    depth = w_ada.shape[0]

    tm_x, tm_c = 512, cl
    tmi_x, tmi_c = 1024, cl
    ts = 256
    tl_x, tl_c = 512, cl // 2
    pool_chunk = 256
    fc = 256

    n_rows = -(-(b + 1) // 8) * 8
    cvec = jnp.zeros((n_rows, d), F32).at[:b].set(c).at[b].set(c_ctx)
    mod = _ada(cvec, w_ada, b_ada)
    gr, gi = _four_weights(w_four)

    cos_x, sin_x = _rope_cos_sin(s, True)
    cos_c, sin_c = _rope_cos_sin(cl, False)
    kt_x = _k_tables(cos_x, sin_x)
    kt_c = _k_tables(cos_c, sin_c)
    dtab_x = _dft_tables(s)
    dtab_c = _dft_tables(cl)
    hm = _block_diag(jnp.full((N_KV_HEADS, HEAD_DIM, HEAD_DIM), 1.0 / HEAD_DIM, F32)).astype(BF16)

    q_end, k_end = Q_WIDTH, Q_WIDTH + KV_WIDTH
    kv_end = k_end + KV_WIDTH
    for i in range(depth):
        update_ctx = i < depth - 1
        wi = w_in[i]
        wqv = jnp.concatenate([wi[:, :q_end], wi[:, k_end:kv_end]], axis=1).T.astype(BF16)
        wr = jnp.concatenate([wi[:, q_end:k_end], wi[:, kv_end:]], axis=1).astype(BF16)
        ab = jnp.concatenate([_block_diag(gr[i]), _block_diag(gi[i])], axis=1).astype(BF16)
        wp = _block_diag(w_pool[i]).astype(BF16)
        ps = pool_scale[i].reshape(1, -1)
        kg = jnp.tile(k_gain[i], N_KV_HEADS).reshape(1, -1)
        wo = w_out[i].astype(BF16)
        wgu = w_gate_up[i].astype(BF16)
        wd = w_down[i].astype(BF16)
        gm = g_mix[i].reshape(1, d)
        gf = g_ffn[i].reshape(1, d)
        mx = [mod[i, :b, j * d:(j + 1) * d].reshape(b, 1, d) for j in range(6)]
        mc = [mod[i, b, j * d:(j + 1) * d].reshape(1, 1, d) for j in range(6)]

        def in_proj(xx, m, cos, sin, kt, tm):
            return _in_proj(xx, m[0], m[1], gm, wqv, wr, hm, kg, _q_table(cos, sin, q_gain[i]),
                            kt[0], kt[1], ab, tm=tm, ts=ts)

        def finish(xx, ot, up, vf, m, dtab, tm, tl):
            yp = _pool(up, wp, ps, chunk=pool_chunk)
            yf = _dft(dtab, vf, tl=tl)
            return _out_ffn(xx, ot, yp, yf, m[2], m[3], m[4], m[5], gf, wo, wgu, wd, tm=tm, fc=fc)

        qt_c, k_c, vt_c, up_c, vf_c = in_proj(ctx, mc, cos_c, sin_c, kt_c, tmi_c)
        qt_x, k_x, vt_x, up_x, vf_x = in_proj(x, mx, cos_x, sin_x, kt_x, tmi_x)
        ot_x = _attention(qt_x, k_c, vt_c, k_x, vt_x, tk=ts)
        x = finish(x, ot_x, up_x, vf_x, mx, dtab_x, tm_x, tl_x)
        if update_ctx:
            ot_c = _attention(qt_c, k_c, vt_c, tk=ts)
            ctx = finish(ctx, ot_c, up_c, vf_c, mc, dtab_c, tm_c, tl_c)
    return x
```

```python
import functools
import math

import jax
import jax.numpy as jnp
from jax import lax
from jax.experimental import pallas as pl
from jax.experimental.pallas import tpu as pltpu

F32 = jnp.float32
BF16 = jnp.bfloat16

GRID_W = 64
HEAD_DIM = 64
N_Q_HEADS = 8
N_KV_HEADS = 2
GQA_GROUP = N_Q_HEADS // N_KV_HEADS
Q_WIDTH = N_Q_HEADS * HEAD_DIM
KV_WIDTH = N_KV_HEADS * HEAD_DIM
POOL_WINDOWS = (2, 4, 8, 16)
POOL_WIDTH = 256
POOL_GROUP = 64
FOUR_WIDTH = 256
N_FREQ = HEAD_DIM // 4
ROPE_THETA = 10000.0
EPS = 1e-6
POOL_HALO = max(POOL_WINDOWS) // 2
LANES = 128
V_ROWS = HEAD_DIM + 16
ATTN_LOOKAHEAD = 5

VMEM_LIMIT_BYTES = 56 * 1024 * 1024


def _params(*sem):
    return pltpu.CompilerParams(dimension_semantics=sem, vmem_limit_bytes=VMEM_LIMIT_BYTES)


def _resident(shape):
    zeros = (0,) * len(shape)
    return pl.BlockSpec(shape, lambda *_: zeros, pipeline_mode=pl.Buffered(1))


def _rms(x):
    return x * lax.rsqrt(jnp.mean(x * x, axis=-1, keepdims=True) + EPS)


def _ada_kernel(c_ref, w_ref, b_ref, o_ref):
    c = c_ref[...]
    sc = c / (1.0 + jnp.exp(-c))
    o_ref[0] = jnp.dot(sc, w_ref[0], preferred_element_type=F32,
                       precision=lax.Precision.HIGHEST) + b_ref[0]


def _ada(cvec, w_ada, b_ada):
    depth, d, n = w_ada.shape
    r = cvec.shape[0]
    tn = 1536
    return pl.pallas_call(
        _ada_kernel,
        out_shape=jax.ShapeDtypeStruct((depth, r, n), F32),
        grid=(depth, n // tn),
        in_specs=[pl.BlockSpec((r, d), lambda i, j: (0, 0)),
                  pl.BlockSpec((1, d, tn), lambda i, j: (i, 0, j)),
                  pl.BlockSpec((1, 1, tn), lambda i, j: (i, 0, j))],
        out_specs=pl.BlockSpec((1, r, tn), lambda i, j: (i, 0, j)),
        compiler_params=_params("arbitrary", "arbitrary"),
        name="ada_mod",
    )(cvec, w_ada, b_ada.reshape(depth, 1, n))


def _four_w_kernel(c_ref, s_ref, w_ref, gr_ref, gi_ref):
    for g in range(w_ref.shape[1]):
        w = w_ref[0, g]
        gr_ref[0, g] = jnp.dot(c_ref[...], w, preferred_element_type=F32, precision=lax.Precision.HIGHEST)
        gi_ref[0, g] = -jnp.dot(s_ref[...], w, preferred_element_type=F32, precision=lax.Precision.HIGHEST)


def _four_weights(w_four):
    depth, ng, gc, _ = w_four.shape
    c = jnp.arange(gc, dtype=jnp.int32)
    ang = ((c[:, None] * c[None, :]) % gc).astype(F32) * (2.0 * math.pi / gc)
    scale = gc ** -0.5
    spec = pl.BlockSpec((1, ng, gc, gc), lambda i: (i, 0, 0, 0))
    tab = pl.BlockSpec((gc, gc), lambda i: (0, 0))
    gr, gi = pl.pallas_call(
        _four_w_kernel,
        out_shape=[jax.ShapeDtypeStruct(w_four.shape, F32)] * 2,
        grid=(depth,),
        in_specs=[tab, tab, spec],
        out_specs=[spec, spec],
        compiler_params=_params("arbitrary"),
        name="four_weights",
    )(jnp.cos(ang) * scale, jnp.sin(ang) * scale, w_four)
    return gr, gi


def _block_diag(blocks):
    g, a, b = blocks.shape
    eye = jnp.eye(g, dtype=blocks.dtype)
    return (eye[:, None, :, None] * blocks[:, :, None, :]).reshape(g * a, g * b)


def _in_proj_kernel(x_ref, sh_ref, sc_ref, g_ref, wqv_ref, wr_ref, hm_ref, kg_ref, qtab_ref,
                    kcos_ref, ksin_ref, ab_ref, q_ref, k_ref, v_ref, up_ref, vf_ref, vf_sc, *, ts):
    tm = x_ref.shape[1]

    def project(j):
        rows = slice(j * ts, (j + 1) * ts)
        h = _rms(x_ref[0, rows, :]) * g_ref[...]
        h = h * (1.0 + sc_ref[0]) + sh_ref[0]
        hb = h.astype(BF16)
        zt = lax.dot_general(wqv_ref[...], hb, (((1,), (1,)), ((), ())), preferred_element_type=F32)
        zr = jnp.dot(hb, wr_ref[...], preferred_element_type=F32)
        return zt, zr

    def emit(j, zt, zr):
        rows = slice(j * ts, (j + 1) * ts)
        tab = qtab_ref[:, rows]
        for hd in range(N_Q_HEADS):
            qh = zt[hd * HEAD_DIM:(hd + 1) * HEAD_DIM, :]
            ss = jnp.sum(qh * qh, axis=0, keepdims=True) * (1.0 / HEAD_DIM)
            qn = qh * lax.rsqrt(ss + EPS)
            parts = []
            for seg in range(2):
                x1 = qn[seg * 32:seg * 32 + N_FREQ]
                x2 = qn[seg * 32 + N_FREQ:seg * 32 + 2 * N_FREQ]
                t0 = seg * 64
                g1c, g2s = tab[t0:t0 + 16], tab[t0 + 16:t0 + 32]
                g2c, g1s = tab[t0 + 32:t0 + 48], tab[t0 + 48:t0 + 64]
                parts += [x1 * g1c - x2 * g2s, x2 * g2c + x1 * g1s]
            g = hd // GQA_GROUP
            o = 1 - g
            q_ref[0, j, hd, g * HEAD_DIM:(g + 1) * HEAD_DIM, :] = jnp.concatenate(parts, axis=0).astype(BF16)
            q_ref[0, j, hd, o * HEAD_DIM:(o + 1) * HEAD_DIM, :] = jnp.zeros((HEAD_DIM, ts), BF16)

        vt = zt[Q_WIDTH:Q_WIDTH + KV_WIDTH].astype(BF16)
        for g in range(N_KV_HEADS):
            v_ref[0, g, j, 0:HEAD_DIM, :] = vt[g * HEAD_DIM:(g + 1) * HEAD_DIM, :]
            v_ref[0, g, j, HEAD_DIM:, :] = jnp.ones((V_ROWS - HEAD_DIM, ts), BF16)

        kz = zr[:, :KV_WIDTH]
        ssk = jnp.dot((kz * kz).astype(BF16), hm_ref[...], preferred_element_type=F32)
        kn = kz * lax.rsqrt(ssk + EPS) * kg_ref[...]
        lane = lax.broadcasted_iota(jnp.int32, kn.shape, 1)
        rot = jnp.where((lane % 32) < N_FREQ,
                        pltpu.roll(kn, KV_WIDTH - N_FREQ, 1), pltpu.roll(kn, N_FREQ, 1))
        k_ref[0, rows, :] = (kn * kcos_ref[rows, :] + rot * ksin_ref[rows, :]).astype(BF16)

        up_ref[0, rows, :] = zr[:, KV_WIDTH:KV_WIDTH + POOL_WIDTH]
        uf = zr[:, KV_WIDTH + POOL_WIDTH:].astype(BF16)
        v2 = jnp.dot(uf, ab_ref[...], preferred_element_type=F32)
        n_chunks = 2 * FOUR_WIDTH // LANES
        per_part = FOUR_WIDTH // LANES
        half = slice(j * ts // 2, (j + 1) * ts // 2)
        for ch in range(n_chunks):
            vf_sc[ch] = v2[:, ch * LANES:(ch + 1) * LANES]
        for par in range(2):
            for ch in range(n_chunks):
                picked = vf_sc[ch, pl.ds(par, ts // 2, stride=2), :]
                part, col = divmod(ch, per_part)
                vf_ref[0, par, part, half, col * LANES:(col + 1) * LANES] = picked.astype(BF16)

    pending = project(0)
    for j in range(tm // ts):
        following = project(j + 1) if (j + 1) * ts < tm else None
        emit(j, *pending)
        pending = following


def _in_proj(x, sh, sc, g, wqv, wr, hm, kg, qtab, kcos, ksin, ab, *, tm, ts):
    tq = tk = ts
    b, length, d = x.shape
    nt = length // tm
    bm = sh.shape[0]
    mod = pl.BlockSpec((1, 1, d), (lambda i, t: (i, 0, 0)) if bm > 1 else (lambda i, t: (0, 0, 0)))
    out_shape = [
        jax.ShapeDtypeStruct((b, length // tq, N_Q_HEADS, KV_WIDTH, tq), BF16),
        jax.ShapeDtypeStruct((b, length, KV_WIDTH), BF16),
        jax.ShapeDtypeStruct((b, N_KV_HEADS, length // tk, V_ROWS, tk), BF16),
        jax.ShapeDtypeStruct((b, length, POOL_WIDTH), F32),
        jax.ShapeDtypeStruct((b, 2, 2, length // 2, FOUR_WIDTH), BF16),
    ]
    out_specs = [
        pl.BlockSpec((1, tm // tq, N_Q_HEADS, KV_WIDTH, tq), lambda i, t: (i, t, 0, 0, 0)),
        pl.BlockSpec((1, tm, KV_WIDTH), lambda i, t: (i, t, 0)),
        pl.BlockSpec((1, N_KV_HEADS, tm // tk, V_ROWS, tk), lambda i, t: (i, 0, t, 0, 0)),
        pl.BlockSpec((1, tm, POOL_WIDTH), lambda i, t: (i, t, 0)),
        pl.BlockSpec((1, 2, 2, tm // 2, FOUR_WIDTH), lambda i, t: (i, 0, 0, t, 0)),
    ]
    in_specs = [
        pl.BlockSpec((1, tm, d), lambda i, t: (i, t, 0)),
        mod, mod,
        _resident((1, d)),
        _resident(wqv.shape), _resident(wr.shape), _resident(hm.shape), _resident(kg.shape),
        pl.BlockSpec((2 * HEAD_DIM, tm), lambda i, t: (0, t)),
        pl.BlockSpec((tm, KV_WIDTH), lambda i, t: (t, 0)),
        pl.BlockSpec((tm, KV_WIDTH), lambda i, t: (t, 0)),
        _resident(ab.shape),
    ]
    return pl.pallas_call(
        functools.partial(_in_proj_kernel, ts=ts),
        out_shape=out_shape, grid=(b, nt), in_specs=in_specs, out_specs=out_specs,
        compiler_params=_params("parallel", "parallel"),
        scratch_shapes=[pltpu.VMEM((2 * FOUR_WIDTH // LANES, ts, LANES), F32)],
        name="in_proj",
    )(x, sh, sc, g, wqv, wr, hm, kg, qtab, kcos, ksin, ab)


def _attn_kernel(*refs, tk, n_ctx, n_lat, ahead):
    if n_lat:
        q_ref, kc_ref, vc_ref, k_ref, v_ref, o_ref, m_sc, acc_sc = refs
    else:
        q_ref, kc_ref, vc_ref, o_ref, m_sc, acc_sc = refs
    m_sc[...] = jnp.full(m_sc.shape, -jnp.inf, F32)
    acc_sc[...] = jnp.zeros(acc_sc.shape, F32)

    def kv_block(c):
        if c < n_ctx:
            return kc_ref[0, c * tk:(c + 1) * tk, :], [vc_ref[0, g, c] for g in range(N_KV_HEADS)]
        c -= n_ctx
        return k_ref[0, c * tk:(c + 1) * tk, :], [v_ref[0, g, c] for g in range(N_KV_HEADS)]

    n_blocks = n_ctx + n_lat
    blocks = [kv_block(c) for c in range(n_blocks)]
    items = [(c, hd) for c in range(n_blocks) for hd in range(N_Q_HEADS)]
    scores = {}

    def issue(i):
        c, hd = items[i]
        scores[i] = jnp.dot(blocks[c][0], q_ref[0, 0, hd], preferred_element_type=F32)

    for i in range(min(ahead, len(items))):
        issue(i)
    for i, (c, hd) in enumerate(items):
        if i + ahead < len(items):
            issue(i + ahead)
        s = scores.pop(i)
        m_old = m_sc[hd]
        m_new = jnp.maximum(m_old, jnp.max(s, axis=0, keepdims=True))
        alpha = jnp.exp2(m_old - m_new)
        p = jnp.exp2(s - m_new)
        acc_sc[hd] = alpha * acc_sc[hd] + jnp.dot(blocks[c][1][hd // GQA_GROUP], p.astype(BF16),
                                                  preferred_element_type=F32)
        m_sc[hd] = m_new

    for hd in range(N_Q_HEADS):
        o = acc_sc[hd, 0:HEAD_DIM, :] * (1.0 / acc_sc[hd, HEAD_DIM:HEAD_DIM + 1, :])
        o_ref[0, hd * HEAD_DIM:(hd + 1) * HEAD_DIM, :] = o.astype(BF16)


def _attention(qt, kc, vc, k=None, v=None, *, tk):
    b, nq, _, _, tq = qt.shape
    length = nq * tq
    n_ctx = kc.shape[1] // tk
    n_lat = 0 if k is None else k.shape[1] // tk
    in_specs = [
        pl.BlockSpec((1, 1) + qt.shape[2:], lambda i, j: (i, j, 0, 0, 0)),
        pl.BlockSpec((1,) + kc.shape[1:], lambda i, j: (i, 0, 0)),
        pl.BlockSpec((1,) + vc.shape[1:], lambda i, j: (i, 0, 0, 0, 0)),
    ]
    args = [qt, kc, vc]
    if n_lat:
        in_specs += [pl.BlockSpec((1,) + k.shape[1:], lambda i, j: (i, 0, 0)),
                     pl.BlockSpec((1,) + v.shape[1:], lambda i, j: (i, 0, 0, 0, 0))]
        args += [k, v]
    return pl.pallas_call(
        functools.partial(_attn_kernel, tk=tk, n_ctx=n_ctx, n_lat=n_lat, ahead=ATTN_LOOKAHEAD),
        out_shape=jax.ShapeDtypeStruct((b, Q_WIDTH, length), BF16),
        grid=(b, nq),
        in_specs=in_specs,
        out_specs=pl.BlockSpec((1, Q_WIDTH, tq), lambda i, j: (i, 0, j)),
        scratch_shapes=[pltpu.VMEM((N_Q_HEADS, 1, tq), F32), pltpu.VMEM((N_Q_HEADS, V_ROWS, tq), F32)],
        compiler_params=_params("parallel", "parallel"),
        name="attention",
    )(*args)


def _pool_kernel(u_ref, w_ref, s_ref, o_ref, pad_ref, *, chunk):
    length = u_ref.shape[1]
    width = u_ref.shape[2]
    hw = POOL_HALO
    pad_ref[0:hw, :] = jnp.zeros((hw, width), F32)
    pad_ref[hw:hw + length, :] = u_ref[0]
    pad_ref[hw + length:, :] = jnp.zeros((hw, width), F32)
    assert 2 * POOL_GROUP == LANES
    first = lax.broadcasted_iota(jnp.int32, (chunk, LANES), 1) < POOL_GROUP
    row = lax.broadcasted_iota(jnp.int32, (chunk, LANES), 0)
    for c in range(length // chunk):
        r0 = c * chunk
        clipped = r0 < hw or r0 + chunk + hw > length
        cols = []
        for col in range(width // LANES):
            wins = POOL_WINDOWS[2 * col:2 * col + 2]
            cl = slice(col * LANES, (col + 1) * LANES)
            if wins[1] < hw:
                sums = []
                acc = None
                lo, hi = hw, hw
                for w in wins:
                    for off in list(range(hw - w // 2, lo)) + list(range(hi, hw + w // 2)):
                        piece = pad_ref[r0 + off:r0 + off + chunk, cl]
                        acc = piece if acc is None else acc + piece
                    lo, hi = hw - w // 2, hw + w // 2
                    sums.append(acc)
            else:
                rows = chunk + 2 * hw
                run = pad_ref[r0:r0 + rows, cl]
                w = 1
                while w < wins[0]:
                    run = run + pltpu.roll(run, rows - w, 0)
                    w *= 2
                assert (w, 2 * w) == tuple(wins) and 2 * w == 2 * hw
                sums = [pltpu.roll(run, rows - (hw - w // 2), 0)[0:chunk],
                        run[0:chunk] + run[w:w + chunk]]
            s = jnp.where(first, sums[0], sums[1])
            if clipped:
                win = jnp.where(first, wins[0], wins[1])
                t = r0 + row
                cnt = jnp.minimum(t + win - win // 2, length) - jnp.maximum(t - win // 2, 0)
                mean = s / cnt.astype(F32)
            else:
                mean = s * jnp.where(first, 1.0 / wins[0], 1.0 / wins[1])
            cols.append(mean - pad_ref[r0 + hw:r0 + hw + chunk, cl])
        d = jnp.concatenate(cols, axis=1)
        y = jnp.dot(d.astype(BF16), w_ref[...], preferred_element_type=F32) * s_ref[...]
        o_ref[0, r0:r0 + chunk, :] = y.astype(BF16)


def _pool(u, w_bd, scale, *, chunk):
    b, length, width = u.shape
    return pl.pallas_call(
        functools.partial(_pool_kernel, chunk=chunk),
        out_shape=jax.ShapeDtypeStruct((b, length, width), BF16),
        grid=(b,),
        in_specs=[pl.BlockSpec((1, length, width), lambda i: (i, 0, 0)),
                  _resident(w_bd.shape), _resident(scale.shape)],
        out_specs=pl.BlockSpec((1, length, width), lambda i: (i, 0, 0)),
        scratch_shapes=[pltpu.VMEM((length + 2 * POOL_HALO, width), F32)],
        compiler_params=_params("parallel"),
        name="pool_mix",
    )(u, w_bd, scale)


def _dft_kernel(te_ref, to_ref, v_ref, o_ref):
    e = jnp.dot(te_ref[...], v_ref[0, 0], preferred_element_type=F32)
    o = jnp.dot(to_ref[...], v_ref[0, 1], preferred_element_type=F32)
    o_ref[0, 0] = (e + o).astype(BF16)
    o_ref[0, 1] = (e - o).astype(BF16)


def _dft(tables, vf, *, tl):
    b, _, _, half, width = vf.shape
    v2 = vf.reshape(b, 2, 2 * half, width)
    tab = pl.BlockSpec((tl, 2 * half), lambda i, j: (i, 0))
    y = pl.pallas_call(
        _dft_kernel,
        out_shape=jax.ShapeDtypeStruct((b, 2, half, width), BF16),
        grid=(half // tl, b),
        in_specs=[tab, tab, pl.BlockSpec((1, 2, 2 * half, width), lambda i, j: (j, 0, 0, 0))],
        out_specs=pl.BlockSpec((1, 2, tl, width), lambda i, j: (j, 0, i, 0)),
        compiler_params=_params("parallel", "parallel"),
        name="seq_dft",
    )(tables[0], tables[1], v2)
    return y.reshape(b, 2 * half, width)


def _dft_tables(length):
    half = length // 2
    nc = 64
    na = half // nc
    w = 2.0 * math.pi / length
    k = jnp.arange(half, dtype=jnp.int32)[:, None]
    a = jnp.arange(na, dtype=jnp.int32)[None, :]
    c = jnp.arange(nc, dtype=jnp.int32)[None, :]
    alpha = ((k * (2 * nc * a)) % length).astype(F32) * w
    ca, sa = jnp.cos(alpha)[:, :, None], jnp.sin(alpha)[:, :, None]
    tables = []
    for par in range(2):
        beta = ((k * (2 * c + par)) % length).astype(F32) * w
        cb, sb = jnp.cos(beta)[:, None, :], jnp.sin(beta)[:, None, :]
        cs = (ca * cb - sa * sb).reshape(half, half)
        sn = (sa * cb + ca * sb).reshape(half, half)
        tables.append((jnp.concatenate([cs, sn], axis=1) * (length ** -0.5)).astype(BF16))
    return tables


def _out_ffn_kernel(x_ref, ot_ref, yp_ref, yf_ref, ga1_ref, sh2_ref, sc2_ref, ga2_ref, g_ref,
                    wo_ref, wgu_ref, wd_ref, o_ref, *, fc):
    d_ff = wd_ref.shape[0]
    mix = lax.dot_general(ot_ref[0], wo_ref[0:Q_WIDTH, :], (((0,), (0,)), ((), ())),
                          preferred_element_type=F32)
    mix += jnp.dot(yp_ref[0], wo_ref[Q_WIDTH:Q_WIDTH + POOL_WIDTH, :], preferred_element_type=F32)
    mix += jnp.dot(yf_ref[0], wo_ref[Q_WIDTH + POOL_WIDTH:, :], preferred_element_type=F32)
    x1 = x_ref[0] + ga1_ref[0] * mix
    f = _rms(x1) * g_ref[...]
    fb = (f * (1.0 + sc2_ref[0]) + sh2_ref[0]).astype(BF16)
    acc = jnp.zeros(x1.shape, F32)
    for c in range(d_ff // fc):
        gate = jnp.dot(fb, wgu_ref[:, c * fc:(c + 1) * fc], preferred_element_type=F32)
        up = jnp.dot(fb, wgu_ref[:, d_ff + c * fc:d_ff + (c + 1) * fc], preferred_element_type=F32)
        hc = (gate / (1.0 + jnp.exp(-gate)) * up).astype(BF16)
        acc += jnp.dot(hc, wd_ref[c * fc:(c + 1) * fc, :], preferred_element_type=F32)
    o_ref[0] = x1 + ga2_ref[0] * acc


def _out_ffn(x, ot, yp, yf, ga1, sh2, sc2, ga2, g, wo, wgu, wd, *, tm, fc):
    b, length, d = x.shape
    bm = ga1.shape[0]
    mod = pl.BlockSpec((1, 1, d), (lambda i, t: (i, 0, 0)) if bm > 1 else (lambda i, t: (0, 0, 0)))
    tok = lambda w: pl.BlockSpec((1, tm, w), lambda i, t: (i, t, 0))
    return pl.pallas_call(
        functools.partial(_out_ffn_kernel, fc=fc),
        out_shape=jax.ShapeDtypeStruct(x.shape, F32),
        grid=(b, length // tm),
        in_specs=[tok(d), pl.BlockSpec((1, Q_WIDTH, tm), lambda i, t: (i, 0, t)),
                  tok(POOL_WIDTH), tok(FOUR_WIDTH), mod, mod, mod, mod, _resident((1, d)),
                  _resident(wo.shape), _resident(wgu.shape), _resident(wd.shape)],
        out_specs=tok(d),
        compiler_params=_params("parallel", "parallel"),
        name="out_ffn",
    )(x, ot, yp, yf, ga1, sh2, sc2, ga2, g, wo, wgu, wd)


def _rope_cos_sin(length, rotate):
    if not rotate:
        return jnp.ones((length, 2, N_FREQ), F32), jnp.zeros((length, 2, N_FREQ), F32)
    rows = length // GRID_W
    row_ids = jnp.repeat(jnp.arange(rows), GRID_W).astype(F32)
    col_ids = jnp.tile(jnp.arange(GRID_W), rows).astype(F32)
    freqs = ROPE_THETA ** (-jnp.arange(N_FREQ, dtype=F32) / N_FREQ)
    ang = jnp.stack([row_ids[:, None] * freqs, col_ids[:, None] * freqs], axis=1)
    return jnp.cos(ang), jnp.sin(ang)


def _q_table(cos, sin, q_gain):
    gq = (q_gain * (HEAD_DIM ** -0.5 * math.log2(math.e))).reshape(2, 2, N_FREQ)
    c = jnp.transpose(cos, (1, 2, 0))
    s = jnp.transpose(sin, (1, 2, 0))
    g1, g2 = gq[:, 0, :, None], gq[:, 1, :, None]
    tab = jnp.stack([g1 * c, g2 * s, g2 * c, g1 * s], axis=1)
    return tab.reshape(2 * HEAD_DIM, -1)


def _k_tables(cos, sin):
    c = jnp.concatenate([cos, cos], axis=2).reshape(cos.shape[0], HEAD_DIM)
    s = jnp.concatenate([-sin, sin], axis=2).reshape(sin.shape[0], HEAD_DIM)
    return jnp.tile(c, (1, N_KV_HEADS)), jnp.tile(s, (1, N_KV_HEADS))


def kernel(x, c, ctx, c_ctx, w_ada, b_ada, g_mix, g_ffn, w_in, q_gain, k_gain, w_pool, pool_scale,
           w_four, w_out, w_gate_up, w_down):
    b, s, d = x.shape
    cl = ctx.shape[1]
    depth = w_ada.shape[0]

    tm_x, tm_c = 512, cl
    tmi_x, tmi_c = 2048, cl
    ts = 256
    tl_x, tl_c = 1024, cl // 2
    pool_chunk = 256
    fc = 256

    n_rows = -(-(b + 1) // 8) * 8
    cvec = jnp.zeros((n_rows, d), F32).at[:b].set(c).at[b].set(c_ctx)
    mod = _ada(cvec, w_ada, b_ada)
    gr, gi = _four_weights(w_four)

    cos_x, sin_x = _rope_cos_sin(s, True)
    cos_c, sin_c = _rope_cos_sin(cl, False)
    kt_x = _k_tables(cos_x, sin_x)
    kt_c = _k_tables(cos_c, sin_c)
    dtab_x = _dft_tables(s)
    dtab_c = _dft_tables(cl)
    hm = _block_diag(jnp.full((N_KV_HEADS, HEAD_DIM, HEAD_DIM), 1.0 / HEAD_DIM, F32)).astype(BF16)

    q_end, k_end = Q_WIDTH, Q_WIDTH + KV_WIDTH
    kv_end = k_end + KV_WIDTH
    for i in range(depth):
        update_ctx = i < depth - 1
        wi = w_in[i]
        wqv = jnp.concatenate([wi[:, :q_end], wi[:, k_end:kv_end]], axis=1).T.astype(BF16)
        wr = jnp.concatenate([wi[:, q_end:k_end], wi[:, kv_end:]], axis=1).astype(BF16)
        ab = jnp.concatenate([_block_diag(gr[i]), _block_diag(gi[i])], axis=1).astype(BF16)
        wp = _block_diag(w_pool[i]).astype(BF16)
        ps = pool_scale[i].reshape(1, -1)
        kg = jnp.tile(k_gain[i], N_KV_HEADS).reshape(1, -1)
        wo = w_out[i].astype(BF16)
        wgu = w_gate_up[i].astype(BF16)
        wd = w_down[i].astype(BF16)
        gm = g_mix[i].reshape(1, d)
        gf = g_ffn[i].reshape(1, d)
        mx = [mod[i, :b, j * d:(j + 1) * d].reshape(b, 1, d) for j in range(6)]
        mc = [mod[i, b, j * d:(j + 1) * d].reshape(1, 1, d) for j in range(6)]

        def in_proj(xx, m, cos, sin, kt, tm):
            return _in_proj(xx, m[0], m[1], gm, wqv, wr, hm, kg, _q_table(cos, sin, q_gain[i]),
                            kt[0], kt[1], ab, tm=tm, ts=ts)

        def finish(xx, ot, up, vf, m, dtab, tm, tl):
            yp = _pool(up, wp, ps, chunk=pool_chunk)
            yf = _dft(dtab, vf, tl=tl)
            return _out_ffn(xx, ot, yp, yf, m[2], m[3], m[4], m[5], gf, wo, wgu, wd, tm=tm, fc=fc)

        qt_c, k_c, vt_c, up_c, vf_c = in_proj(ctx, mc, cos_c, sin_c, kt_c, tmi_c)
        qt_x, k_x, vt_x, up_x, vf_x = in_proj(x, mx, cos_x, sin_x, kt_x, tmi_x)
        ot_x = _attention(qt_x, k_c, vt_c, k_x, vt_x, tk=ts)
        x = finish(x, ot_x, up_x, vf_x, mx, dtab_x, tm_x, tl_x)
        if update_ctx:
            ot_c = _attention(qt_c, k_c, vt_c, tk=ts)
            ctx = finish(ctx, ot_c, up_c, vf_c, mc, dtab_c, tm_c, tl_c)
    return x
```
